```python
import jax, jax.numpy as jnp
from jax import lax
import numpy as np

D_MODEL = 1024
BATCH = 2
SEQ = 16384
DEPTH = 1
DEC_BATCH = 8
DEC_SEQ = 16
PAST_LEN = 4096

CHUNK = 64
Q_BLOCK = 128
H_A = 4
K_A = 128
V_A = 128
D_A = H_A * K_A
D_AV = H_A * V_A
H_B = 8
D_HB = 64
D_B = H_B * D_HB
D_FF = 2816
N_IN = 2 * D_A + 2 * D_AV + 3 * D_B + 2 * D_MODEL
_SPLITS = (D_A, 2 * D_A, 2 * D_A + D_AV, 2 * D_A + 2 * D_AV,
           2 * D_A + 2 * D_AV + D_B, 2 * D_A + 2 * D_AV + 2 * D_B,
           2 * D_A + 2 * D_AV + 3 * D_B, 2 * D_A + 2 * D_AV + 3 * D_B + D_MODEL)
ALPHA = (2 * DEPTH) ** 0.25
BETA_INIT = (8 * DEPTH) ** -0.25
SB_SCALE = D_HB ** -0.5
LN_EPS = 1e-5
RMS_EPS = 1e-6

kernel_name = 'hgrn2_stickbreaking_macaron_deepnorm_step'


def _layer_norm(x, g, b):
    xf = x.astype(jnp.float32)
    mu = jnp.mean(xf, axis=-1, keepdims=True)
    var = jnp.mean(jnp.square(xf - mu), axis=-1, keepdims=True)
    y = (xf - mu) * lax.rsqrt(var + LN_EPS) * g.astype(jnp.float32) + b.astype(jnp.float32)
    return y.astype(x.dtype)


def _swiglu(x, wg, wu, wd):
    return (jax.nn.silu(x @ wg) * (x @ wu)) @ wd


def _hgrn2_scan(q, k, v, log_f, s0):
    bsz, t_len = q.shape[:2]
    c_len = CHUNK if t_len >= CHUNK else t_len
    n_chunks = t_len // c_len

    def to_chunks(a):
        a = a.astype(jnp.float32).reshape(bsz, n_chunks, c_len, a.shape[2], a.shape[3])
        return jnp.moveaxis(a, 1, 0)

    qc, kc, vc, fc = to_chunks(q), to_chunks(k), to_chunks(v), to_chunks(log_f)
    bc = lax.cumsum(fc, axis=2)
    incl = jnp.tril(jnp.ones((c_len, c_len), dtype=bool))[None, :, :, None, None]

    def step(state, inp):
        qt, kt, vt, bt = inp
        o_inter = jnp.einsum('bthk,bhkv->bthv', qt * jnp.exp(bt), state)
        diff = bt[:, :, None] - bt[:, None, :]
        decay = jnp.where(incl, jnp.exp(jnp.where(incl, diff, 0.0)), 0.0)
        scores = jnp.einsum('bthk,btshk,bshk->bhts', qt, decay, kt)
        o_intra = jnp.einsum('bhts,bshv->bthv', scores, vt)
        b_last = bt[:, -1]
        k_dec = kt * jnp.exp(b_last[:, None] - bt)
        new_state = jnp.exp(b_last)[..., None] * state + jnp.einsum('bshk,bshv->bhkv', k_dec, vt)
        return new_state, o_inter + o_intra

    s_fin, o = lax.scan(step, s0.astype(jnp.float32), (qc, kc, vc, bc))
    o = jnp.moveaxis(o, 0, 1).reshape(bsz, t_len, H_A, V_A)
    return o, s_fin


def _sb_block(q, k, v, q_pos, k_pos):
    z = jnp.einsum('bthd,bshd->bhts', q, k).astype(jnp.float32) * SB_SCALE
    causal = k_pos[None, :] < q_pos[:, None]
    log_surv = jnp.where(causal, jax.nn.log_sigmoid(-z), 0.0)
    after = lax.cumsum(log_surv, axis=3, reverse=True) - log_surv
    w = jnp.where(causal, jnp.exp(jax.nn.log_sigmoid(z) + after), 0.0)
    return jnp.einsum('bhts,bshd->bthd', w, v.astype(jnp.float32))


def _stick_breaking(q, k, v, q_start):
    bsz, t_len = q.shape[:2]
    k_pos = jnp.arange(k.shape[1], dtype=jnp.int32)
    if t_len <= Q_BLOCK:
        return _sb_block(q, k, v, q_start + jnp.arange(t_len, dtype=jnp.int32), k_pos)
    n_blk = t_len // Q_BLOCK
    q_blocks = jnp.moveaxis(q.reshape(bsz, n_blk, Q_BLOCK, H_B, D_HB), 1, 0)

    def one_block(args):
        blk, q_blk = args
        q_pos = q_start + blk * Q_BLOCK + jnp.arange(Q_BLOCK, dtype=jnp.int32)
        return _sb_block(q_blk, k, v, q_pos, k_pos)

    o = lax.map(one_block, (jnp.arange(n_blk, dtype=jnp.int32), q_blocks))
    return jnp.moveaxis(o, 0, 1).reshape(bsz, t_len, H_B, D_HB)


def _token_mixer(x, s0, past_k, past_v, lb, w_in, hgrn_norm_g, w_branch_a, w_branch_b, w_out):
    bsz, t_len, _ = x.shape
    q_a, f_a, i_a, g_a, q_b, k_b, v_b, gate_a, gate_b = jnp.split(x @ w_in, _SPLITS, axis=-1)

    lb_h = lb.reshape(H_A, K_A)
    z_f = f_a.astype(jnp.float32).reshape(bsz, t_len, H_A, K_A)
    log_f = jnp.logaddexp(jnp.log(lb_h), jnp.log1p(-lb_h) + jax.nn.log_sigmoid(z_f))
    hk = -jnp.expm1(log_f)
    hq = jax.nn.silu(q_a).reshape(bsz, t_len, H_A, K_A)
    hv = i_a.reshape(bsz, t_len, H_A, V_A)
    o_a, s_new = _hgrn2_scan(hq, hk, hv, log_f, s0)
    o_a = o_a * lax.rsqrt(jnp.mean(jnp.square(o_a), axis=-1, keepdims=True) + RMS_EPS)
    o_a = o_a * hgrn_norm_g.astype(jnp.float32).reshape(H_A, V_A)
    h_a = o_a.reshape(bsz, t_len, D_AV).astype(x.dtype) * jax.nn.silu(g_a)

    qb = q_b.reshape(bsz, t_len, H_B, D_HB)
    kb = k_b.reshape(bsz, t_len, H_B, D_HB)
    vb = v_b.reshape(bsz, t_len, H_B, D_HB)
    if past_k is None:
        k_all, v_all, q_start = kb, vb, 0
    else:
        k_all = jnp.concatenate([past_k.astype(kb.dtype), kb], axis=1)
        v_all = jnp.concatenate([past_v.astype(vb.dtype), vb], axis=1)
        q_start = past_k.shape[1]
    h_b = _stick_breaking(qb, k_all, v_all, q_start).reshape(bsz, t_len, D_B).astype(x.dtype)

    merged = jax.nn.sigmoid(gate_a) * (h_a @ w_branch_a) + jax.nn.sigmoid(gate_b) * (h_b @ w_branch_b)
    return merged @ w_out, kb, vb, s_new


def _trunk(x, s0, past_k, past_v, weights):
    (ffn1_wg, ffn1_wu, ffn1_wd, ln1_g, ln1_b, w_in, lb_logits, hgrn_norm_g,
     w_branch_a, w_branch_b, w_out, ln2_g, ln2_b,
     ffn2_wg, ffn2_wu, ffn2_wd, ln3_g, ln3_b) = weights
    lower_bounds = lax.cumsum(jax.nn.softmax(lb_logits.astype(jnp.float32), axis=0), axis=0)
    ks, vs, ss = [], [], []
    for l in range(DEPTH):
        x = _layer_norm(ALPHA * x + 0.5 * _swiglu(x, ffn1_wg[l], ffn1_wu[l], ffn1_wd[l]), ln1_g[l], ln1_b[l])
        mix, k_new, v_new, s_new = _token_mixer(
            x, s0[l],
            None if past_k is None else past_k[l],
            None if past_v is None else past_v[l],
            lower_bounds[l], w_in[l], hgrn_norm_g[l], w_branch_a[l], w_branch_b[l], w_out[l])
        x = _layer_norm(ALPHA * x + mix, ln2_g[l], ln2_b[l])
        x = _layer_norm(ALPHA * x + 0.5 * _swiglu(x, ffn2_wg[l], ffn2_wu[l], ffn2_wd[l]), ln3_g[l], ln3_b[l])
        ks.append(k_new)
        vs.append(v_new)
        ss.append(s_new)
    return x, jnp.stack(ks), jnp.stack(vs), jnp.stack(ss)


def setup_inputs(seed: int = 0) -> dict:
    key = jax.random.key(seed)
    ks = jax.random.split(key, 24)

    def nrm(k, shape, scale):
        return scale * jax.random.normal(k, shape, jnp.float32)

    d_in = D_MODEL ** -0.5
    d_ff = D_FF ** -0.5
    return {
        'x_prompt': nrm(ks[0], (BATCH, SEQ, D_MODEL), 1.0),
        'x_sample': nrm(ks[1], (DEC_BATCH, DEC_SEQ, D_MODEL), 1.0),
        'cache_sb_k': nrm(ks[2], (DEPTH, DEC_BATCH, PAST_LEN, H_B, D_HB), 1.0),
        'cache_sb_v': nrm(ks[3], (DEPTH, DEC_BATCH, PAST_LEN, H_B, D_HB), 1.0),
        'state_hgrn': nrm(ks[4], (DEPTH, DEC_BATCH, H_A, K_A, V_A), 0.5),
        'ffn1_wg': nrm(ks[5], (DEPTH, D_MODEL, D_FF), d_in),
        'ffn1_wu': nrm(ks[6], (DEPTH, D_MODEL, D_FF), d_in * BETA_INIT),
        'ffn1_wd': nrm(ks[7], (DEPTH, D_FF, D_MODEL), d_ff * BETA_INIT),
        'ln1_g': 1.0 + nrm(ks[8], (DEPTH, D_MODEL), 0.02),
        'ln1_b': nrm(ks[9], (DEPTH, D_MODEL), 0.02),
        'w_in': nrm(ks[10], (DEPTH, D_MODEL, N_IN), d_in),
        'lb_logits': nrm(ks[11], (DEPTH + 1, D_A), 0.5),
        'hgrn_norm_g': 1.0 + nrm(ks[12], (DEPTH, D_AV), 0.02),
        'w_branch_a': nrm(ks[13], (DEPTH, D_AV, D_MODEL), (D_AV ** -0.5) * BETA_INIT),
        'w_branch_b': nrm(ks[14], (DEPTH, D_B, D_MODEL), (D_B ** -0.5) * BETA_INIT),
        'w_out': nrm(ks[15], (DEPTH, D_MODEL, D_MODEL), d_in * BETA_INIT),
        'ln2_g': 1.0 + nrm(ks[16], (DEPTH, D_MODEL), 0.02),
        'ln2_b': nrm(ks[17], (DEPTH, D_MODEL), 0.02),
        'ffn2_wg': nrm(ks[18], (DEPTH, D_MODEL, D_FF), d_in),
        'ffn2_wu': nrm(ks[19], (DEPTH, D_MODEL, D_FF), d_in * BETA_INIT),
        'ffn2_wd': nrm(ks[20], (DEPTH, D_FF, D_MODEL), d_ff * BETA_INIT),
        'ln3_g': 1.0 + nrm(ks[21], (DEPTH, D_MODEL), 0.02),
        'ln3_b': nrm(ks[22], (DEPTH, D_MODEL), 0.02),
    }


def reference(x_prompt, x_sample, cache_sb_k, cache_sb_v, state_hgrn,
              ffn1_wg, ffn1_wu, ffn1_wd, ln1_g, ln1_b, w_in, lb_logits, hgrn_norm_g,
              w_branch_a, w_branch_b, w_out, ln2_g, ln2_b,
              ffn2_wg, ffn2_wu, ffn2_wd, ln3_g, ln3_b):
    weights = (ffn1_wg, ffn1_wu, ffn1_wd, ln1_g, ln1_b, w_in, lb_logits, hgrn_norm_g,
               w_branch_a, w_branch_b, w_out, ln2_g, ln2_b,
               ffn2_wg, ffn2_wu, ffn2_wd, ln3_g, ln3_b)
    s0_prompt = jnp.zeros((DEPTH, x_prompt.shape[0], H_A, K_A, V_A), jnp.float32)
    y_prompt, k_prompt, v_prompt, s_prompt = _trunk(x_prompt, s0_prompt, None, None, weights)
    y_sample, k_sample, v_sample, s_sample = _trunk(x_sample, state_hgrn, cache_sb_k, cache_sb_v, weights)
    return (y_prompt, y_sample, k_prompt, v_prompt, s_prompt, k_sample, v_sample, s_sample)
```

```python
import functools

import jax
import jax.numpy as jnp
from jax import lax
from jax.experimental import pallas as pl
from jax.experimental.pallas import tpu as pltpu

F32 = jnp.float32
BF16 = jnp.bfloat16

D_MODEL = 1024
H_A, K_A, V_A = 4, 128, 128
D_A = H_A * K_A
D_AV = H_A * V_A
H_B, D_HB = 8, 64
D_B = H_B * D_HB
D_FF = 2816
N_IN = 2 * D_A + 2 * D_AV + 3 * D_B + 2 * D_MODEL
LN_EPS = 1e-5
RMS_EPS = 1e-6
SB_SCALE = D_HB ** -0.5

V7X_LANES = 128
V7X_SUBLANES = 8
V7X_MXU_DIM = 256
V7X_VMEM_LIMIT_BYTES = 56 * 1024 * 1024

FF_CHUNK = V7X_MXU_DIM
HGRN_CHUNK = 16
SB_TK = 256
SB_NV = SB_TK // V7X_SUBLANES


def _params(sem):
    return pltpu.CompilerParams(dimension_semantics=sem, vmem_limit_bytes=V7X_VMEM_LIMIT_BYTES)


def _resident(shape):
    nd = len(shape)
    return pl.BlockSpec(shape, lambda *_: (0,) * nd, pipeline_mode=pl.Buffered(1))


def _layer_norm(y, g, b):
    mu = jnp.mean(y, axis=-1, keepdims=True)
    d = y - mu
    var = jnp.mean(d * d, axis=-1, keepdims=True)
    return d * lax.rsqrt(var + LN_EPS) * g + b


def _silu(x):
    return x * jax.nn.sigmoid(x)


def _swiglu(xb, wg_ref, wu_ref, wd_ref):
    acc = jnp.zeros((xb.shape[0], D_MODEL), F32)
    for c in range(D_FF // FF_CHUNK):
        sl = slice(c * FF_CHUNK, (c + 1) * FF_CHUNK)
        g = jnp.dot(xb, wg_ref[:, sl], preferred_element_type=F32)
        u = jnp.dot(xb, wu_ref[:, sl], preferred_element_type=F32)
        h = (_silu(g) * u).astype(BF16)
        acc = acc + jnp.dot(h, wd_ref[sl, :], preferred_element_type=F32)
    return acc


def _ffn_ln_kernel(x_ref, wg_ref, wu_ref, wd_ref, g_ref, b_ref, o_ref, *, alpha):
    x = x_ref[...]
    ff = _swiglu(x.astype(BF16), wg_ref, wu_ref, wd_ref)
    o_ref[...] = _layer_norm(alpha * x + 0.5 * ff, g_ref[...], b_ref[...])


def _ffn_ln(x, wg, wu, wd, g, b, *, alpha, tm):
    n = x.shape[0]
    row = pl.BlockSpec((tm, D_MODEL), lambda i: (i, 0))
    return pl.pallas_call(
        functools.partial(_ffn_ln_kernel, alpha=alpha),
        grid=(n // tm,),
        in_specs=[row, _resident(wg.shape), _resident(wu.shape), _resident(wd.shape),
                  _resident(g.shape), _resident(b.shape)],
        out_specs=row,
        out_shape=jax.ShapeDtypeStruct((n, D_MODEL), F32),
        compiler_params=_params(("parallel",)),
        name="ffn_ln",
    )(x, wg, wu, wd, g, b)


_CUTS = (0, D_A, 2 * D_A, 2 * D_A + D_AV, 2 * D_A + 2 * D_AV,
         2 * D_A + 2 * D_AV + D_B, 2 * D_A + 2 * D_AV + 2 * D_B,
         2 * D_A + 2 * D_AV + 3 * D_B, 2 * D_A + 2 * D_AV + 3 * D_B + D_MODEL, N_IN)


def _inproj_kernel(x_ref, w_ref, lbl_ref, hq_ref, hk_ref, lf_ref, hv_ref, sg_ref,
                   qb_ref, kb_ref, vb_ref, ga_ref, gb_ref, *, layer):
    xb = x_ref[...].astype(BF16)

    def proj(i):
        return jnp.dot(xb, w_ref[:, _CUTS[i]:_CUTS[i + 1]], preferred_element_type=F32)

    hq_ref[...] = _silu(proj(0))
    rows = [lbl_ref[i:i + 1, :] for i in range(lbl_ref.shape[0])]
    m = functools.reduce(jnp.maximum, rows)
    ex = [jnp.exp(r - m) for r in rows]
    one_minus_lb = sum(ex[layer + 1:]) / sum(ex)
    hk = one_minus_lb * jax.nn.sigmoid(-proj(1))
    hk_ref[...] = hk
    lf_ref[...] = jnp.log1p(-hk)
    hv_ref[...] = proj(2)
    sg_ref[...] = _silu(proj(3))
    qb_ref[...] = (proj(4) * SB_SCALE).astype(BF16)
    kb_ref[...] = proj(5)
    vb_ref[...] = proj(6)
    ga_ref[...] = jax.nn.sigmoid(proj(7))
    gb_ref[...] = jax.nn.sigmoid(proj(8))


def _inproj(x1, w_in, lb_logits, *, layer, tm):
    n = x1.shape[0]

    def row(width):
        return pl.BlockSpec((tm, width), lambda i: (i, 0))

    widths = (D_A, D_A, D_A, D_AV, D_AV, D_B, D_B, D_B, D_MODEL, D_MODEL)
    dtypes = (F32, F32, F32, F32, F32, BF16, F32, F32, F32, F32)
    return pl.pallas_call(
        functools.partial(_inproj_kernel, layer=layer),
        grid=(n // tm,),
        in_specs=[row(D_MODEL), _resident(w_in.shape), _resident(lb_logits.shape)],
        out_specs=[row(w) for w in widths],
        out_shape=[jax.ShapeDtypeStruct((n, w), d) for w, d in zip(widths, dtypes)],
        compiler_params=_params(("parallel",)),
        name="inproj",
    )(x1, w_in, lb_logits)


def _cumsum_rows(x):
    n = x.shape[0]
    row = lax.broadcasted_iota(jnp.int32, x.shape, 0)
    sh = 1
    while sh < n:
        x = x + jnp.where(row >= sh, pltpu.roll(x, sh, axis=0), 0.0)
        sh *= 2
    return x


def _hgrn_kernel(q_ref, k_ref, lf_ref, v_ref, sg_ref, gn_ref, s0_ref, o_ref, sout_ref, st_scr, *, tb):
    c = HGRN_CHUNK
    t_blk = pl.program_id(1)

    @pl.when(t_blk == 0)
    def _():
        for h in range(H_A):
            st_scr[h] = s0_ref[0, h].T

    row = lax.broadcasted_iota(jnp.int32, (c, K_A), 0)

    def chunk(i, carry):
        r0 = pl.multiple_of(i * c, c)
        q4 = q_ref[pl.ds(r0, c), :]
        k4 = k_ref[pl.ds(r0, c), :]
        lf4 = lf_ref[pl.ds(r0, c), :]
        v4 = v_ref[pl.ds(r0, c), :]
        sg4 = sg_ref[pl.ds(r0, c), :]
        outs = []
        for h in range(H_A):
            sl = slice(h * K_A, (h + 1) * K_A)
            q, k, v = q4[:, sl], k4[:, sl], v4[:, sl]
            b = _cumsum_rows(lf4[:, sl])
            b_last = b[c - 1:c, :]
            st = st_scr[h]
            o = lax.dot_general((q * jnp.exp(b)).astype(BF16), st.astype(BF16),
                                (((1,), (1,)), ((), ())), preferred_element_type=F32)
            for s in range(c):
                vis = row >= s
                w = jnp.where(vis, q * jnp.exp(jnp.where(vis, b - b[s:s + 1, :], 0.0)) * k[s:s + 1, :], 0.0)
                o = o + jnp.sum(w, axis=-1, keepdims=True) * v[s:s + 1, :]
            kd = (k * jnp.exp(b_last - b)).astype(BF16)
            upd = lax.dot_general(v.astype(BF16), kd, (((0,), (0,)), ((), ())),
                                  preferred_element_type=F32)
            st_scr[h] = st * jnp.exp(b_last) + upd
            o = o * lax.rsqrt(jnp.mean(o * o, axis=-1, keepdims=True) + RMS_EPS)
            outs.append(o * gn_ref[:, sl] * sg4[:, sl])
        o_ref[pl.ds(r0, c), :] = jnp.concatenate(outs, axis=1).astype(o_ref.dtype)
        return carry

    lax.fori_loop(0, tb // c, chunk, 0)

    @pl.when(t_blk == pl.num_programs(1) - 1)
    def _():
        for h in range(H_A):
            sout_ref[0, h] = st_scr[h].T


def _hgrn(hq, hk, lf, hv, sg, gn, s0, *, bsz, t_len, tb):
    nt = t_len // tb
    row = pl.BlockSpec((tb, D_A), lambda b, t: (b * nt + t, 0))
    state = pl.BlockSpec((1, H_A, K_A, V_A), lambda b, t: (b, 0, 0, 0))
    return pl.pallas_call(
        functools.partial(_hgrn_kernel, tb=tb),
        grid=(bsz, nt),
        in_specs=[row, row, row, row, row, _resident(gn.shape), state],
        out_specs=[row, state],
        out_shape=[jax.ShapeDtypeStruct((bsz * t_len, D_AV), BF16),
                   jax.ShapeDtypeStruct((bsz, H_A, K_A, V_A), F32)],
        scratch_shapes=[pltpu.VMEM((H_A, V_A, K_A), F32)],
        compiler_params=_params(("parallel", "arbitrary")),
        name="hgrn",
    )(hq, hk, lf, hv, sg, gn, s0)


def _shift_up(x, sh):
    row = lax.broadcasted_iota(jnp.int32, x.shape, 0)
    return jnp.where(row < V7X_SUBLANES - sh, pltpu.roll(x, V7X_SUBLANES - sh, axis=0), 1.0)


def _sb_block(kb, vt, qt, carry, *, q_off):
    masked = q_off is not None
    tq = qt.shape[1]
    z = jnp.dot(kb, qt, preferred_element_type=F32)
    g = 1.0 / (1.0 + jnp.exp(z))
    run = jnp.ones((V7X_SUBLANES, tq), F32)
    diffs = [None] * SB_NV
    if masked:
        sub = lax.broadcasted_iota(jnp.int32, (V7X_SUBLANES, tq), 0)
        lane = lax.broadcasted_iota(jnp.int32, (V7X_SUBLANES, tq), 1) + q_off
    for v in reversed(range(SB_NV)):
        gv = g[v * V7X_SUBLANES:(v + 1) * V7X_SUBLANES, :]
        if masked:
            gv = jnp.where(sub * SB_NV + v < lane, gv, 1.0)
        nxt = run * gv
        diffs[v] = run - nxt
        run = nxt
    y = _shift_up(run, 1)
    y = y * _shift_up(y, 1)
    y = y * _shift_up(y, 2)
    y = y * _shift_up(y, 4)
    offs = carry * y
    w = jnp.concatenate([d * offs for d in diffs], axis=0).astype(BF16)
    new_carry = jnp.broadcast_to((offs * run)[0:1, :], carry.shape)
    return jnp.dot(vt, w, preferred_element_type=F32), new_carry


def _sb_kernel(qt_ref, k_ref, vt_ref, o_ref, *, tq, q_start):
    q0 = q_start + pl.program_id(2) * tq
    n_full = q0 // SB_TK
    qt2 = qt_ref[0]
    half = lax.broadcasted_iota(jnp.int32, qt2.shape, 0) < D_HB
    zero = jnp.zeros_like(qt2)
    qts = (jnp.where(half, qt2, zero), jnp.where(half, zero, qt2))

    def block(j, carries, acc, masked):
        s0 = pl.multiple_of(j * SB_TK, SB_TK)
        kb = k_ref[0, pl.ds(s0, SB_TK), :]
        new_carries, parts = [], []
        for a in range(2):
            vt = vt_ref[0, a * D_HB:(a + 1) * D_HB, pl.ds(s0, SB_TK)]
            part, nc = _sb_block(kb, vt, qts[a], carries[a], q_off=q0 - s0 if masked else None)
            parts.append(part)
            new_carries.append(nc)
        return tuple(new_carries), acc + jnp.concatenate(parts, axis=0)

    ones = jnp.ones((V7X_SUBLANES, tq), F32)
    carries, acc = block(n_full, (ones, ones), jnp.zeros((2 * D_HB, tq), F32), True)

    def body(jj, state):
        return block(n_full - 1 - jj, state[0], state[1], False)

    carries, acc = lax.fori_loop(0, n_full, body, (carries, acc))
    o_ref[...] = acc.T.astype(o_ref.dtype)


def _sb(qt, k_perm, vt_perm, *, tq, q_start):
    bsz, _, t_q = qt.shape
    s_len = k_perm.shape[1]
    nq = t_q // tq
    assert tq <= SB_TK and q_start % SB_TK == 0 and SB_TK % tq == 0
    assert s_len % SB_TK == 0 and s_len >= q_start + nq * tq - tq + SB_TK
    pair = 2 * D_HB
    return pl.pallas_call(
        functools.partial(_sb_kernel, tq=tq, q_start=q_start),
        grid=(bsz, H_B // 2, nq),
        in_specs=[pl.BlockSpec((1, pair, tq), lambda b, h, i: (b, h, i)),
                  pl.BlockSpec((1, s_len, pair), lambda b, h, i: (b, 0, h)),
                  pl.BlockSpec((1, pair, s_len), lambda b, h, i: (b, h, 0))],
        out_specs=pl.BlockSpec((tq, pair), lambda b, h, i: (b * nq + i, h)),
        out_shape=jax.ShapeDtypeStruct((bsz * t_q, D_B), BF16),
        compiler_params=_params(("parallel", "parallel", "arbitrary")),
        name="sb",
    )(qt, k_perm, vt_perm)


def _permute_keys(x):
    bsz, s_len, d = x.shape
    x = x.reshape(bsz, s_len // SB_TK, V7X_SUBLANES, SB_NV, d)
    return jnp.swapaxes(x, 2, 3).reshape(bsz, s_len, d)


def _mix_ffn_kernel(x_ref, ha_ref, hb_ref, ga_ref, gb_ref, wa_ref, wb_ref, wo_ref, g2_ref, b2_ref,
                    wg_ref, wu_ref, wd_ref, g3_ref, b3_ref, o_ref, *, alpha):
    pa = jnp.dot(ha_ref[...], wa_ref[...], preferred_element_type=F32)
    pb = jnp.dot(hb_ref[...], wb_ref[...], preferred_element_type=F32)
    merged = ga_ref[...] * pa + gb_ref[...] * pb
    mix = jnp.dot(merged.astype(BF16), wo_ref[...], preferred_element_type=F32)
    x2 = _layer_norm(alpha * x_ref[...] + mix, g2_ref[...], b2_ref[...])
    ff = _swiglu(x2.astype(BF16), wg_ref, wu_ref, wd_ref)
    o_ref[...] = _layer_norm(alpha * x2 + 0.5 * ff, g3_ref[...], b3_ref[...])


def _mix_ffn(x1, ha, hb, ga, gb, wa, wb, wo, g2, b2, wg, wu, wd, g3, b3, *, alpha, tm):
    n = x1.shape[0]

    def row(width):
        return pl.BlockSpec((tm, width), lambda i: (i, 0))

    consts = (wa, wb, wo, g2, b2, wg, wu, wd, g3, b3)
    return pl.pallas_call(
        functools.partial(_mix_ffn_kernel, alpha=alpha),
        grid=(n // tm,),
        in_specs=[row(D_MODEL), row(D_AV), row(D_B), row(D_MODEL), row(D_MODEL)]
                 + [_resident(c.shape) for c in consts],
        out_specs=row(D_MODEL),
        out_shape=jax.ShapeDtypeStruct((n, D_MODEL), F32),
        compiler_params=_params(("parallel",)),
        name="mix_ffn",
    )(x1, ha, hb, ga, gb, *consts)


def _pick_tile(n, cap):
    t = min(n, cap)
    assert n % t == 0
    return t


def _trunk(x, s0, past_k, past_v, w):
    bsz, t_len, _ = x.shape
    n = bsz * t_len
    depth = w["w_in"].shape[0]
    alpha = (2 * depth) ** 0.25
    tm = _pick_tile(n, 512)
    xf = x.reshape(n, D_MODEL)
    ks, vs, ss = [], [], []
    for l in range(depth):
        x1 = _ffn_ln(xf, w["ffn1_wg"][l], w["ffn1_wu"][l], w["ffn1_wd"][l], w["ln1_g"][l], w["ln1_b"][l],
                     alpha=alpha, tm=tm)
        hq, hk, lf, hv, sg, qb, kb, vb, ga, gb = _inproj(x1, w["w_in"][l], w["lb_logits"], layer=l, tm=tm)
        ha, s_new = _hgrn(hq, hk, lf, hv, sg, w["hgrn_norm_g"][l], s0[l],
                          bsz=bsz, t_len=t_len, tb=_pick_tile(t_len, 512))

        k3 = kb.astype(BF16).reshape(bsz, t_len, D_B)
        v3 = vb.astype(BF16).reshape(bsz, t_len, D_B)
        q3 = qb.reshape(bsz, t_len, D_B)
        if past_k is None:
            q_start, tq = 0, SB_TK
            t_pad = t_len
        else:
            q_start = past_k.shape[2]
            tq = V7X_LANES
            t_pad = tq
            assert t_len <= tq
            k_pad = jnp.zeros((bsz, SB_TK - t_len, D_B), BF16)
            k3 = jnp.concatenate([past_k[l].astype(BF16).reshape(bsz, q_start, D_B), k3, k_pad], axis=1)
            v3 = jnp.concatenate([past_v[l].astype(BF16).reshape(bsz, q_start, D_B), v3, k_pad], axis=1)
            q3 = jnp.pad(q3, ((0, 0), (0, t_pad - t_len), (0, 0)))
        qt = jnp.swapaxes(q3, 1, 2)
        k_perm = _permute_keys(k3)
        vt_perm = jnp.swapaxes(_permute_keys(v3), 1, 2)
        hb = _sb(qt, k_perm, vt_perm, tq=tq, q_start=q_start)
        if t_pad != t_len:
            hb = hb.reshape(bsz, t_pad, D_B)[:, :t_len].reshape(n, D_B)

        xf = _mix_ffn(x1, ha, hb, ga, gb, w["w_branch_a"][l], w["w_branch_b"][l], w["w_out"][l],
                      w["ln2_g"][l], w["ln2_b"][l], w["ffn2_wg"][l], w["ffn2_wu"][l], w["ffn2_wd"][l],
                      w["ln3_g"][l], w["ln3_b"][l], alpha=alpha, tm=tm)
        ks.append(kb.reshape(bsz, t_len, H_B, D_HB))
        vs.append(vb.reshape(bsz, t_len, H_B, D_HB))
        ss.append(s_new)
    return xf.reshape(bsz, t_len, D_MODEL), jnp.stack(ks), jnp.stack(vs), jnp.stack(ss)


def kernel(x_prompt, x_sample, cache_sb_k, cache_sb_v, state_hgrn, ffn1_wg, ffn1_wu, ffn1_wd, ln1_g, ln1_b,
           w_in, lb_logits, hgrn_norm_g, w_branch_a, w_branch_b, w_out, ln2_g, ln2_b,
           ffn2_wg, ffn2_wu, ffn2_wd, ln3_g, ln3_b):
    depth = w_in.shape[0]

    def vec(p):
        return p[:, None, :]

    w = dict(
        ffn1_wg=ffn1_wg.astype(BF16), ffn1_wu=ffn1_wu.astype(BF16), ffn1_wd=ffn1_wd.astype(BF16),
        ln1_g=vec(ln1_g), ln1_b=vec(ln1_b), w_in=w_in.astype(BF16), lb_logits=lb_logits,
        hgrn_norm_g=vec(hgrn_norm_g), w_branch_a=w_branch_a.astype(BF16), w_branch_b=w_branch_b.astype(BF16),
        w_out=w_out.astype(BF16), ln2_g=vec(ln2_g), ln2_b=vec(ln2_b),
        ffn2_wg=ffn2_wg.astype(BF16), ffn2_wu=ffn2_wu.astype(BF16), ffn2_wd=ffn2_wd.astype(BF16),
        ln3_g=vec(ln3_g), ln3_b=vec(ln3_b))
    s0_prompt = jnp.zeros((depth, x_prompt.shape[0], H_A, K_A, V_A), F32)
    y_p, k_p, v_p, s_p = _trunk(x_prompt, s0_prompt, None, None, w)
    y_s, k_s, v_s, s_s = _trunk(x_sample, state_hgrn, cache_sb_k, cache_sb_v, w)
    return (y_p, y_s, k_p, v_p, s_p, k_s, v_s, s_s)
```

```python
import functools

import jax
import jax.numpy as jnp
from jax import lax
from jax.experimental import pallas as pl
from jax.experimental.pallas import tpu as pltpu

F32 = jnp.float32
BF16 = jnp.bfloat16

D_MODEL = 1024
H_A, K_A, V_A = 4, 128, 128
D_A = H_A * K_A
D_AV = H_A * V_A
H_B, D_HB = 8, 64
D_B = H_B * D_HB
D_FF = 2816
N_IN = 2 * D_A + 2 * D_AV + 3 * D_B + 2 * D_MODEL
LN_EPS = 1e-5
RMS_EPS = 1e-6
SB_SCALE = D_HB ** -0.5

V7X_LANES = 128
V7X_SUBLANES = 8
V7X_MXU_DIM = 256
V7X_VMEM_LIMIT_BYTES = 56 * 1024 * 1024

FF_CHUNK = V7X_MXU_DIM
HGRN_CHUNK = 16
SB_TK = 256
SB_NV = SB_TK // V7X_SUBLANES


def _params(sem):
    return pltpu.CompilerParams(dimension_semantics=sem, vmem_limit_bytes=V7X_VMEM_LIMIT_BYTES)


def _resident(shape):
    nd = len(shape)
    return pl.BlockSpec(shape, lambda *_: (0,) * nd, pipeline_mode=pl.Buffered(1))


def _layer_norm(y, g, b):
    mu = jnp.mean(y, axis=-1, keepdims=True)
    d = y - mu
    var = jnp.mean(d * d, axis=-1, keepdims=True)
    return d * lax.rsqrt(var + LN_EPS) * g + b


def _silu(x):
    return x * jax.nn.sigmoid(x)


def _swiglu(xb, wg_ref, wu_ref, wd_ref):
    acc = jnp.zeros((xb.shape[0], D_MODEL), F32)
    for c in range(D_FF // FF_CHUNK):
        sl = slice(c * FF_CHUNK, (c + 1) * FF_CHUNK)
        g = jnp.dot(xb, wg_ref[:, sl], preferred_element_type=F32)
        u = jnp.dot(xb, wu_ref[:, sl], preferred_element_type=F32)
        h = (_silu(g) * u).astype(BF16)
        acc = acc + jnp.dot(h, wd_ref[sl, :], preferred_element_type=F32)
    return acc


def _ffn_ln_kernel(x_ref, wg_ref, wu_ref, wd_ref, g_ref, b_ref, o_ref, *, alpha):
    x = x_ref[...]
    ff = _swiglu(x.astype(BF16), wg_ref, wu_ref, wd_ref)
    o_ref[...] = _layer_norm(alpha * x + 0.5 * ff, g_ref[...], b_ref[...])


def _ffn_ln(x, wg, wu, wd, g, b, *, alpha, tm):
    n = x.shape[0]
    row = pl.BlockSpec((tm, D_MODEL), lambda i: (i, 0))
    return pl.pallas_call(
        functools.partial(_ffn_ln_kernel, alpha=alpha),
        grid=(n // tm,),
        in_specs=[row, _resident(wg.shape), _resident(wu.shape), _resident(wd.shape),
                  _resident(g.shape), _resident(b.shape)],
        out_specs=row,
        out_shape=jax.ShapeDtypeStruct((n, D_MODEL), F32),
        compiler_params=_params(("parallel",)),
        name="ffn_ln",
    )(x, wg, wu, wd, g, b)


_CUTS = (0, D_A, 2 * D_A, 2 * D_A + D_AV, 2 * D_A + 2 * D_AV,
         2 * D_A + 2 * D_AV + D_B, 2 * D_A + 2 * D_AV + 2 * D_B,
         2 * D_A + 2 * D_AV + 3 * D_B, 2 * D_A + 2 * D_AV + 3 * D_B + D_MODEL, N_IN)


def _key_permutation(transposed):
    i = lax.broadcasted_iota(jnp.int32, (SB_TK, SB_TK), 1 if transposed else 0)
    j = lax.broadcasted_iota(jnp.int32, (SB_TK, SB_TK), 0 if transposed else 1)
    src = (i % V7X_SUBLANES) * SB_NV + i // V7X_SUBLANES
    return jnp.where(j == src, 1.0, 0.0).astype(BF16)


def _inproj_kernel(x_ref, w_ref, lbl_ref, hq_ref, hk_ref, lf_ref, hv_ref, sg_ref,
                   kb_ref, vb_ref, ga_ref, gb_ref, *sb_refs, layer, sb_layout):
    xb = x_ref[...].astype(BF16)

    def proj(i):
        return jnp.dot(xb, w_ref[:, _CUTS[i]:_CUTS[i + 1]], preferred_element_type=F32)

    hq_ref[...] = _silu(proj(0))
    rows = [lbl_ref[i:i + 1, :] for i in range(lbl_ref.shape[0])]
    m = functools.reduce(jnp.maximum, rows)
    ex = [jnp.exp(r - m) for r in rows]
    one_minus_lb = sum(ex[layer + 1:]) / sum(ex)
    hk = one_minus_lb * jax.nn.sigmoid(-proj(1))
    hk_ref[...] = hk
    lf_ref[...] = jnp.log1p(-hk)
    hv_ref[...] = proj(2)
    sg_ref[...] = _silu(proj(3))
    ga_ref[...] = jax.nn.sigmoid(proj(7))
    gb_ref[...] = jax.nn.sigmoid(proj(8))
    qh = proj(4) * (0.5 * SB_SCALE)
    kf = proj(5)
    vf = proj(6)
    kb_ref[...] = kf
    vb_ref[...] = vf
    if not sb_layout:
        sb_refs[0][...] = qh.astype(BF16)
        return
    qt_ref, kp_ref, vt_ref = sb_refs
    qt_ref[0] = qh.T.astype(BF16)
    perm = _key_permutation(False)
    perm_t = _key_permutation(True)
    for blk in range(x_ref.shape[0] // SB_TK):
        rows = slice(blk * SB_TK, (blk + 1) * SB_TK)
        kp_ref[rows, :] = jnp.dot(perm, kf[rows].astype(BF16), preferred_element_type=F32).astype(BF16)
        vt_ref[0, :, rows] = jnp.dot(vf[rows].T.astype(BF16), perm_t, preferred_element_type=F32).astype(BF16)


def _inproj(x1, w_in, lb_logits, *, layer, tm, t_len, sb_layout):
    n = x1.shape[0]
    nt = t_len // tm

    def row(width):
        return pl.BlockSpec((tm, width), lambda i: (i, 0))

    cols = pl.BlockSpec((1, D_B, tm), lambda i: (i // nt, 0, i % nt))
    widths = (D_A, D_A, D_A, D_AV, D_AV, D_B, D_B, D_MODEL, D_MODEL)
    specs = [row(w) for w in widths]
    shapes = [jax.ShapeDtypeStruct((n, w), F32) for w in widths]
    if sb_layout:
        assert tm % SB_TK == 0 and t_len % tm == 0
        specs += [cols, row(D_B), cols]
        shapes += [jax.ShapeDtypeStruct((n // t_len, D_B, t_len), BF16), jax.ShapeDtypeStruct((n, D_B), BF16),
                   jax.ShapeDtypeStruct((n // t_len, D_B, t_len), BF16)]
    else:
        specs += [row(D_B)]
        shapes += [jax.ShapeDtypeStruct((n, D_B), BF16)]
    return pl.pallas_call(
        functools.partial(_inproj_kernel, layer=layer, sb_layout=sb_layout),
        grid=(n // tm,),
        in_specs=[row(D_MODEL), _resident(w_in.shape), _resident(lb_logits.shape)],
        out_specs=specs,
        out_shape=shapes,
        compiler_params=_params(("parallel",)),
        name="inproj",
    )(x1, w_in, lb_logits)


def _cumsum_rows(x):
    n = x.shape[0]
    row = lax.broadcasted_iota(jnp.int32, x.shape, 0)
    sh = 1
    while sh < n:
        x = x + jnp.where(row >= sh, pltpu.roll(x, sh, axis=0), 0.0)
        sh *= 2
    return x


def _hgrn_kernel(q_ref, k_ref, lf_ref, v_ref, sg_ref, gn_ref, s0_ref, o_ref, sout_ref, st_scr, b_scr, *, tb):
    c = HGRN_CHUNK
    t_blk = pl.program_id(1)

    @pl.when(t_blk == 0)
    def _():
        for h in range(H_A):
            st_scr[h] = s0_ref[0, h].T

    half = V7X_SUBLANES
    row = lax.broadcasted_iota(jnp.int32, (half, K_A), 0)
    ones_kk = jnp.ones((K_A, K_A), BF16)

    def chunk(i, carry):
        q4 = q_ref[i]
        k4 = k_ref[i]
        v4 = v_ref[i]
        sg4 = sg_ref[i]
        b4 = _cumsum_rows(lf_ref[i])
        b_scr[...] = b4
        outs = []
        for h in range(H_A):
            sl = slice(h * K_A, (h + 1) * K_A)
            q, k, v, b = q4[:, sl], k4[:, sl], v4[:, sl], b4[:, sl]
            b_last = b[c - 1:c, :]
            st = st_scr[h]
            o = lax.dot_general((q * jnp.exp(b)).astype(BF16), st.astype(BF16),
                                (((1,), (1,)), ((), ())), preferred_element_type=F32)
            o_h = [o[:half], o[half:]]
            mxu_terms = []
            for s in range(c):
                ks = k_ref[i, s:s + 1, sl]
                vs = v_ref[i, s:s + 1, sl]
                bs = b_scr[s:s + 1, sl]
                for p in range(2):
                    lo = p * half
                    if s >= lo + half:
                        continue
                    w = q[lo:lo + half] * jnp.exp(b[lo:lo + half] - bs) * ks
                    if s > lo:
                        w = jnp.where(row >= s - lo, w, 0.0)
                    if (s + p) % 2:
                        mxu_terms.append((p, vs, w))
                    else:
                        o_h[p] = o_h[p] + jnp.sum(w, axis=-1, keepdims=True) * vs
            sums = jnp.dot(jnp.concatenate([t[2] for t in mxu_terms], axis=0).astype(BF16), ones_kk,
                           preferred_element_type=F32)
            for n, (p, vs, _) in enumerate(mxu_terms):
                o_h[p] = o_h[p] + sums[n * half:(n + 1) * half] * vs
            o = jnp.concatenate(o_h, axis=0)
            kd = (k * jnp.exp(b_last - b)).astype(BF16)
            upd = lax.dot_general(v.astype(BF16), kd, (((0,), (0,)), ((), ())),
                                  preferred_element_type=F32)
            st_scr[h] = st * jnp.exp(b_last) + upd
            o = o * lax.rsqrt(jnp.mean(o * o, axis=-1, keepdims=True) + RMS_EPS)
            outs.append(o * gn_ref[:, sl] * sg4[:, sl])
        o_ref[i] = jnp.concatenate(outs, axis=1).astype(o_ref.dtype)
        return carry

    lax.fori_loop(0, tb // c, chunk, 0, unroll=min(2, tb // c))

    @pl.when(t_blk == pl.num_programs(1) - 1)
    def _():
        for h in range(H_A):
            sout_ref[0, h] = st_scr[h].T


def _hgrn(hq, hk, lf, hv, sg, gn, s0, *, bsz, t_len, tb):
    nt = t_len // tb
    c = HGRN_CHUNK
    n = bsz * t_len
    row = pl.BlockSpec((tb // c, c, D_A), lambda b, t: (b * nt + t, 0, 0))
    state = pl.BlockSpec((1, H_A, K_A, V_A), lambda b, t: (b, 0, 0, 0))
    ha, s_new = pl.pallas_call(
        functools.partial(_hgrn_kernel, tb=tb),
        grid=(bsz, nt),
        in_specs=[row, row, row, row, row, _resident(gn.shape), state],
        out_specs=[row, state],
        out_shape=[jax.ShapeDtypeStruct((n // c, c, D_AV), BF16),
                   jax.ShapeDtypeStruct((bsz, H_A, K_A, V_A), F32)],
        scratch_shapes=[pltpu.VMEM((H_A, V_A, K_A), F32), pltpu.VMEM((c, D_A), F32)],
        compiler_params=_params(("parallel", "arbitrary")),
        name="hgrn",
    )(*(a.reshape(n // c, c, D_A) for a in (hq, hk, lf, hv, sg)), gn, s0)
    return ha.reshape(n, D_AV), s_new


def _shift_up(x, sh):
    row = lax.broadcasted_iota(jnp.int32, x.shape, 0)
    return jnp.where(row < V7X_SUBLANES - sh, pltpu.roll(x, V7X_SUBLANES - sh, axis=0), 1.0)


def _sb_weights(z_ref, w_ref, carry, *, q_off=None, scale=None):
    tq = z_ref.shape[1]
    new_carry = []
    for c0 in range(0, tq, V7X_LANES):
        cols = slice(c0, c0 + V7X_LANES)
        if q_off is not None:
            sub = lax.broadcasted_iota(jnp.int32, (V7X_SUBLANES, V7X_LANES), 0) * SB_NV
            lane = lax.broadcasted_iota(jnp.int32, (V7X_SUBLANES, V7X_LANES), 1) + (q_off + c0)
        run = jnp.ones((V7X_SUBLANES, V7X_LANES), F32)
        diffs = [None] * SB_NV
        for v in reversed(range(SB_NV)):
            gv = 0.5 - 0.5 * jnp.tanh(z_ref[v * V7X_SUBLANES:(v + 1) * V7X_SUBLANES, cols])
            if q_off is not None:
                gv = jnp.where(sub + v < lane, gv, 1.0)
            nxt = run * gv
            diffs[v] = run - nxt
            run = nxt
        y = _shift_up(run, 1)
        y = y * _shift_up(y, 1)
        y = y * _shift_up(y, 2)
        y = y * _shift_up(y, 4)
        offs = carry[:, cols] * y
        if scale is not None:
            offs = offs * scale
        for v in range(0, SB_NV, 2):
            pair = jnp.concatenate([diffs[v] * offs, diffs[v + 1] * offs], axis=0)
            w_ref[v * V7X_SUBLANES:(v + 2) * V7X_SUBLANES, cols] = pair.astype(BF16)
        new_carry.append(jnp.broadcast_to((offs * run)[0:1, :], run.shape))
    return jnp.concatenate(new_carry, axis=1)


def _sb_kernel(qt_ref, k_ref, vt_ref, o_ref, z_scr, w_scr, acc_scr, *, tq, q_start):
    q0 = q_start + pl.program_id(2) * tq
    n_full = q0 // SB_TK
    n_pairs = (n_full + 1) // 2
    qt2 = qt_ref[0]
    half = lax.broadcasted_iota(jnp.int32, qt2.shape, 0) < D_HB
    zero = jnp.zeros_like(qt2)
    qts = (jnp.where(half, qt2, zero), jnp.where(half, zero, qt2))

    def scores(j, slot):
        s0 = pl.multiple_of(jnp.maximum(j, 0) * SB_TK, SB_TK)
        kb = k_ref[0, pl.ds(s0, SB_TK), :]
        for a in range(2):
            z_scr[slot, a] = jnp.dot(kb, qts[a], preferred_element_type=F32)

    def values(j, slot):
        s0 = pl.multiple_of(jnp.maximum(j, 0) * SB_TK, SB_TK)
        parts = [jnp.dot(vt_ref[0, a * D_HB:(a + 1) * D_HB, pl.ds(s0, SB_TK)], w_scr[slot, a],
                         preferred_element_type=F32) for a in range(2)]
        acc_scr[...] += jnp.concatenate(parts, axis=0)

    def weights(slot, carries, **kw):
        return tuple(_sb_weights(z_scr.at[slot, a], w_scr.at[slot, a], carries[a], **kw) for a in range(2))

    scores(n_full, 1)
    scores(n_full - 1, 0)
    acc_scr[...] = jnp.zeros_like(acc_scr)
    ones = jnp.ones((V7X_SUBLANES, tq), F32)
    carries = weights(1, (ones, ones), q_off=q0 - n_full * SB_TK)

    def pair(p, carries):
        j = n_full - 1 - 2 * p
        values(j + 1, 1)
        scores(j - 1, 1)
        carries = weights(0, carries)
        scores(j - 2, 0)
        values(j, 0)
        valid = (j >= 1).astype(F32)
        return weights(1, carries, scale=valid)

    lax.fori_loop(0, n_pairs, pair, carries)
    values(n_full - 2 * n_pairs, 1)
    o_ref[...] = acc_scr[...].T.astype(o_ref.dtype)


def _sb(qt, k_perm, vt_perm, *, tq, q_start):
    bsz, _, t_q = qt.shape
    s_len = k_perm.shape[1]
    nq = t_q // tq
    assert tq <= SB_TK and q_start % SB_TK == 0 and SB_TK % tq == 0
    assert s_len % SB_TK == 0 and s_len >= q_start + nq * tq - tq + SB_TK
    pair = 2 * D_HB
    return pl.pallas_call(
        functools.partial(_sb_kernel, tq=tq, q_start=q_start),
        grid=(bsz, H_B // 2, nq),
        in_specs=[pl.BlockSpec((1, pair, tq), lambda b, h, i: (b, h, i)),
                  pl.BlockSpec((1, s_len, pair), lambda b, h, i: (b, 0, h)),
                  pl.BlockSpec((1, pair, s_len), lambda b, h, i: (b, h, 0))],
        out_specs=pl.BlockSpec((tq, pair), lambda b, h, i: (b * nq + i, h)),
        out_shape=jax.ShapeDtypeStruct((bsz * t_q, D_B), BF16),
        scratch_shapes=[pltpu.VMEM((2, 2, SB_TK, tq), F32), pltpu.VMEM((2, 2, SB_TK, tq), BF16),
                        pltpu.VMEM((pair, tq), F32)],
        compiler_params=_params(("parallel", "parallel", "arbitrary")),
        name="sb",
    )(qt, k_perm, vt_perm)


def _permute_keys(x):
    bsz, s_len, d = x.shape
    x = x.reshape(bsz, s_len // SB_TK, V7X_SUBLANES, SB_NV, d)
    return jnp.swapaxes(x, 2, 3).reshape(bsz, s_len, d)


def _mix_ffn_kernel(x_ref, ha_ref, hb_ref, ga_ref, gb_ref, wa_ref, wb_ref, wo_ref, g2_ref, b2_ref,
                    wg_ref, wu_ref, wd_ref, g3_ref, b3_ref, o_ref, *, alpha):
    pa = jnp.dot(ha_ref[...], wa_ref[...], preferred_element_type=F32)
    pb = jnp.dot(hb_ref[...], wb_ref[...], preferred_element_type=F32)
    merged = ga_ref[...] * pa + gb_ref[...] * pb
    mix = jnp.dot(merged.astype(BF16), wo_ref[...], preferred_element_type=F32)
    x2 = _layer_norm(alpha * x_ref[...] + mix, g2_ref[...], b2_ref[...])
    ff = _swiglu(x2.astype(BF16), wg_ref, wu_ref, wd_ref)
    o_ref[...] = _layer_norm(alpha * x2 + 0.5 * ff, g3_ref[...], b3_ref[...])


def _mix_ffn(x1, ha, hb, ga, gb, wa, wb, wo, g2, b2, wg, wu, wd, g3, b3, *, alpha, tm):
    n = x1.shape[0]

    def row(width):
        return pl.BlockSpec((tm, width), lambda i: (i, 0))

    consts = (wa, wb, wo, g2, b2, wg, wu, wd, g3, b3)
    return pl.pallas_call(
        functools.partial(_mix_ffn_kernel, alpha=alpha),
        grid=(n // tm,),
        in_specs=[row(D_MODEL), row(D_AV), row(D_B), row(D_MODEL), row(D_MODEL)]
                 + [_resident(c.shape) for c in consts],
        out_specs=row(D_MODEL),
        out_shape=jax.ShapeDtypeStruct((n, D_MODEL), F32),
        compiler_params=_params(("parallel",)),
        name="mix_ffn",
    )(x1, ha, hb, ga, gb, *consts)


def _pick_tile(n, cap):
    t = min(n, cap)
    assert n % t == 0
    return t


def _trunk(x, s0, past_k, past_v, w):
    bsz, t_len, _ = x.shape
    n = bsz * t_len
    depth = w["w_in"].shape[0]
    alpha = (2 * depth) ** 0.25
    tm = _pick_tile(n, 512)
    xf = x.reshape(n, D_MODEL)
    ks, vs, ss = [], [], []
    for l in range(depth):
        x1 = _ffn_ln(xf, w["ffn1_wg"][l], w["ffn1_wu"][l], w["ffn1_wd"][l], w["ln1_g"][l], w["ln1_b"][l],
                     alpha=alpha, tm=tm)
        sb_layout = past_k is None
        hq, hk, lf, hv, sg, kb, vb, ga, gb, *sb_ops = _inproj(
            x1, w["w_in"][l], w["lb_logits"], layer=l, tm=tm, t_len=t_len, sb_layout=sb_layout)
        ha, s_new = _hgrn(hq, hk, lf, hv, sg, w["hgrn_norm_g"][l], s0[l],
                          bsz=bsz, t_len=t_len, tb=_pick_tile(t_len, 512))

        if sb_layout:
            qt, k_perm, vt_perm = sb_ops
            hb = _sb(qt, k_perm.reshape(bsz, t_len, D_B), vt_perm, tq=SB_TK, q_start=0)
        else:
            q_start = past_k.shape[2]
            tq = V7X_LANES
            assert t_len <= tq
            k_pad = jnp.zeros((bsz, SB_TK - t_len, D_B), BF16)
            k3 = jnp.concatenate([past_k[l].astype(BF16).reshape(bsz, q_start, D_B),
                                  kb.astype(BF16).reshape(bsz, t_len, D_B), k_pad], axis=1)
            v3 = jnp.concatenate([past_v[l].astype(BF16).reshape(bsz, q_start, D_B),
                                  vb.astype(BF16).reshape(bsz, t_len, D_B), k_pad], axis=1)
            q3 = jnp.pad(sb_ops[0].reshape(bsz, t_len, D_B), ((0, 0), (0, tq - t_len), (0, 0)))
            hb = _sb(jnp.swapaxes(q3, 1, 2), _permute_keys(k3), jnp.swapaxes(_permute_keys(v3), 1, 2),
                     tq=tq, q_start=q_start)
            hb = hb.reshape(bsz, tq, D_B)[:, :t_len].reshape(n, D_B)

        xf = _mix_ffn(x1, ha, hb, ga, gb, w["w_branch_a"][l], w["w_branch_b"][l], w["w_out"][l],
                      w["ln2_g"][l], w["ln2_b"][l], w["ffn2_wg"][l], w["ffn2_wu"][l], w["ffn2_wd"][l],
                      w["ln3_g"][l], w["ln3_b"][l], alpha=alpha, tm=tm)
        ks.append(kb.reshape(bsz, t_len, H_B, D_HB))
        vs.append(vb.reshape(bsz, t_len, H_B, D_HB))
        ss.append(s_new)
    return xf.reshape(bsz, t_len, D_MODEL), jnp.stack(ks), jnp.stack(vs), jnp.stack(ss)


def kernel(x_prompt, x_sample, cache_sb_k, cache_sb_v, state_hgrn, ffn1_wg, ffn1_wu, ffn1_wd, ln1_g, ln1_b,
           w_in, lb_logits, hgrn_norm_g, w_branch_a, w_branch_b, w_out, ln2_g, ln2_b,
           ffn2_wg, ffn2_wu, ffn2_wd, ln3_g, ln3_b):
    depth = w_in.shape[0]

    def vec(p):
        return p[:, None, :]

    w = dict(
        ffn1_wg=ffn1_wg.astype(BF16), ffn1_wu=ffn1_wu.astype(BF16), ffn1_wd=ffn1_wd.astype(BF16),
        ln1_g=vec(ln1_g), ln1_b=vec(ln1_b), w_in=w_in.astype(BF16), lb_logits=lb_logits,
        hgrn_norm_g=vec(hgrn_norm_g), w_branch_a=w_branch_a.astype(BF16), w_branch_b=w_branch_b.astype(BF16),
        w_out=w_out.astype(BF16), ln2_g=vec(ln2_g), ln2_b=vec(ln2_b),
        ffn2_wg=ffn2_wg.astype(BF16), ffn2_wu=ffn2_wu.astype(BF16), ffn2_wd=ffn2_wd.astype(BF16),
        ln3_g=vec(ln3_g), ln3_b=vec(ln3_b))
    s0_prompt = jnp.zeros((depth, x_prompt.shape[0], H_A, K_A, V_A), F32)
    y_p, k_p, v_p, s_p = _trunk(x_prompt, s0_prompt, None, None, w)
    y_s, k_s, v_s, s_s = _trunk(x_sample, state_hgrn, cache_sb_k, cache_sb_v, w)
    return (y_p, y_s, k_p, v_p, s_p, k_s, v_s, s_s)
```

```python
import functools

import jax
import jax.numpy as jnp
from jax import lax
from jax.experimental import pallas as pl
from jax.experimental.pallas import tpu as pltpu

F32 = jnp.float32
BF16 = jnp.bfloat16

D_MODEL = 1024
H_A, K_A, V_A = 4, 128, 128
D_A = H_A * K_A
D_AV = H_A * V_A
H_B, D_HB = 8, 64
D_B = H_B * D_HB
D_FF = 2816
N_IN = 2 * D_A + 2 * D_AV + 3 * D_B + 2 * D_MODEL
LN_EPS = 1e-5
RMS_EPS = 1e-6
SB_SCALE = D_HB ** -0.5

V7X_LANES = 128
V7X_SUBLANES = 8
V7X_MXU_DIM = 256
V7X_VMEM_LIMIT_BYTES = 56 * 1024 * 1024

FF_CHUNK = V7X_MXU_DIM
HGRN_CHUNK = 16
SB_TK = 256
SB_NV = SB_TK // V7X_SUBLANES


def _params(sem):
    return pltpu.CompilerParams(dimension_semantics=sem, vmem_limit_bytes=V7X_VMEM_LIMIT_BYTES)


def _resident(shape):
    nd = len(shape)
    return pl.BlockSpec(shape, lambda *_: (0,) * nd, pipeline_mode=pl.Buffered(1))


def _layer_norm(y, g, b):
    mu = jnp.mean(y, axis=-1, keepdims=True)
    d = y - mu
    var = jnp.mean(d * d, axis=-1, keepdims=True)
    return d * lax.rsqrt(var + LN_EPS) * g + b


def _silu(x):
    return x * jax.nn.sigmoid(x)


def _swiglu(xb, wg_ref, wu_ref, wd_ref):
    acc = jnp.zeros((xb.shape[0], D_MODEL), F32)
    for c in range(D_FF // FF_CHUNK):
        sl = slice(c * FF_CHUNK, (c + 1) * FF_CHUNK)
        g = jnp.dot(xb, wg_ref[:, sl], preferred_element_type=F32)
        u = jnp.dot(xb, wu_ref[:, sl], preferred_element_type=F32)
        h = (_silu(g) * u).astype(BF16)
        acc = acc + jnp.dot(h, wd_ref[sl, :], preferred_element_type=F32)
    return acc


def _ffn_ln_kernel(x_ref, wg_ref, wu_ref, wd_ref, g_ref, b_ref, o_ref, *, alpha):
    x = x_ref[...]
    ff = _swiglu(x.astype(BF16), wg_ref, wu_ref, wd_ref)
    o_ref[...] = _layer_norm(alpha * x + 0.5 * ff, g_ref[...], b_ref[...])


def _ffn_ln(x, wg, wu, wd, g, b, *, alpha, tm):
    n = x.shape[0]
    row = pl.BlockSpec((tm, D_MODEL), lambda i: (i, 0))
    return pl.pallas_call(
        functools.partial(_ffn_ln_kernel, alpha=alpha),
        grid=(n // tm,),
        in_specs=[row, _resident(wg.shape), _resident(wu.shape), _resident(wd.shape),
                  _resident(g.shape), _resident(b.shape)],
        out_specs=row,
        out_shape=jax.ShapeDtypeStruct((n, D_MODEL), F32),
        compiler_params=_params(("parallel",)),
        name="ffn_ln",
    )(x, wg, wu, wd, g, b)


_CUTS = (0, D_A, 2 * D_A, 2 * D_A + D_AV, 2 * D_A + 2 * D_AV,
         2 * D_A + 2 * D_AV + D_B, 2 * D_A + 2 * D_AV + 2 * D_B,
         2 * D_A + 2 * D_AV + 3 * D_B, 2 * D_A + 2 * D_AV + 3 * D_B + D_MODEL, N_IN)


def _key_permutation(transposed):
    i = lax.broadcasted_iota(jnp.int32, (SB_TK, SB_TK), 1 if transposed else 0)
    j = lax.broadcasted_iota(jnp.int32, (SB_TK, SB_TK), 0 if transposed else 1)
    src = (i % V7X_SUBLANES) * SB_NV + i // V7X_SUBLANES
    return jnp.where(j == src, 1.0, 0.0).astype(BF16)


def _inproj_kernel(x_ref, w_ref, lbl_ref, hq_ref, hk_ref, lf_ref, hv_ref, sg_ref,
                   kb_ref, vb_ref, ga_ref, gb_ref, *sb_refs, layer, sb_layout):
    xb = x_ref[...].astype(BF16)

    def proj(i):
        return jnp.dot(xb, w_ref[:, _CUTS[i]:_CUTS[i + 1]], preferred_element_type=F32)

    hq_ref[...] = _silu(proj(0))
    rows = [lbl_ref[i:i + 1, :] for i in range(lbl_ref.shape[0])]
    m = functools.reduce(jnp.maximum, rows)
    ex = [jnp.exp(r - m) for r in rows]
    one_minus_lb = sum(ex[layer + 1:]) / sum(ex)
    hk = one_minus_lb * jax.nn.sigmoid(-proj(1))
    hk_ref[...] = hk
    lf_ref[...] = jnp.log1p(-hk)
    hv_ref[...] = proj(2)
    sg_ref[...] = _silu(proj(3))
    ga_ref[...] = jax.nn.sigmoid(proj(7))
    gb_ref[...] = jax.nn.sigmoid(proj(8))
    qh = proj(4) * (0.5 * SB_SCALE)
    kf = proj(5)
    vf = proj(6)
    kb_ref[...] = kf
    vb_ref[...] = vf
    if not sb_layout:
        sb_refs[0][...] = qh.astype(BF16)
        return
    qt_ref, kp_ref, vt_ref = sb_refs
    qt_ref[0] = qh.T.astype(BF16)
    perm = _key_permutation(False)
    perm_t = _key_permutation(True)
    for blk in range(x_ref.shape[0] // SB_TK):
        rows = slice(blk * SB_TK, (blk + 1) * SB_TK)
        kp_ref[rows, :] = jnp.dot(perm, kf[rows].astype(BF16), preferred_element_type=F32).astype(BF16)
        vt_ref[0, :, rows] = jnp.dot(vf[rows].T.astype(BF16), perm_t, preferred_element_type=F32).astype(BF16)


def _inproj(x1, w_in, lb_logits, *, layer, tm, t_len, sb_layout):
    n = x1.shape[0]
    nt = t_len // tm

    def row(width):
        return pl.BlockSpec((tm, width), lambda i: (i, 0))

    cols = pl.BlockSpec((1, D_B, tm), lambda i: (i // nt, 0, i % nt))
    widths = (D_A, D_A, D_A, D_AV, D_AV, D_B, D_B, D_MODEL, D_MODEL)
    specs = [row(w) for w in widths]
    shapes = [jax.ShapeDtypeStruct((n, w), F32) for w in widths]
    if sb_layout:
        assert tm % SB_TK == 0 and t_len % tm == 0
        specs += [cols, row(D_B), cols]
        shapes += [jax.ShapeDtypeStruct((n // t_len, D_B, t_len), BF16), jax.ShapeDtypeStruct((n, D_B), BF16),
                   jax.ShapeDtypeStruct((n // t_len, D_B, t_len), BF16)]
    else:
        specs += [row(D_B)]
        shapes += [jax.ShapeDtypeStruct((n, D_B), BF16)]
    return pl.pallas_call(
        functools.partial(_inproj_kernel, layer=layer, sb_layout=sb_layout),
        grid=(n // tm,),
        in_specs=[row(D_MODEL), _resident(w_in.shape), _resident(lb_logits.shape)],
        out_specs=specs,
        out_shape=shapes,
        compiler_params=_params(("parallel",)),
        name="inproj",
    )(x1, w_in, lb_logits)


def _cumsum_rows(x):
    n = x.shape[0]
    row = lax.broadcasted_iota(jnp.int32, x.shape, 0)
    sh = 1
    while sh < n:
        x = x + jnp.where(row >= sh, pltpu.roll(x, sh, axis=0), 0.0)
        sh *= 2
    return x


def _hgrn_kernel(q_ref, k_ref, lf_ref, v_ref, sg_ref, gn_ref, s0_ref, o_ref, sout_ref, st_scr, b_scr, *, tb):
    c = HGRN_CHUNK
    t_blk = pl.program_id(1)

    @pl.when(t_blk == 0)
    def _():
        for h in range(H_A):
            st_scr[h] = s0_ref[0, h].T

    half = V7X_SUBLANES
    row = lax.broadcasted_iota(jnp.int32, (half, K_A), 0)
    ones_kk = jnp.ones((K_A, K_A), BF16)

    def chunk(i, carry):
        q4 = q_ref[i]
        k4 = k_ref[i]
        v4 = v_ref[i]
        sg4 = sg_ref[i]
        b4 = _cumsum_rows(lf_ref[i])
        b_scr[...] = b4
        outs = []
        for h in range(H_A):
            sl = slice(h * K_A, (h + 1) * K_A)
            q, k, v, b = q4[:, sl], k4[:, sl], v4[:, sl], b4[:, sl]
            b_last = b[c - 1:c, :]
            st = st_scr[h]
            o = lax.dot_general((q * jnp.exp(b)).astype(BF16), st.astype(BF16),
                                (((1,), (1,)), ((), ())), preferred_element_type=F32)
            o_h = [o[:half], o[half:]]
            mxu_terms = []
            for s in range(c):
                ks = k_ref[i, s:s + 1, sl]
                vs = v_ref[i, s:s + 1, sl]
                bs = b_scr[s:s + 1, sl]
                for p in range(2):
                    lo = p * half
                    if s >= lo + half:
                        continue
                    w = q[lo:lo + half] * jnp.exp(b[lo:lo + half] - bs) * ks
                    if s > lo:
                        w = jnp.where(row >= s - lo, w, 0.0)
                    if (s + p) % 2:
                        mxu_terms.append((p, vs, w))
                    else:
                        o_h[p] = o_h[p] + jnp.sum(w, axis=-1, keepdims=True) * vs
            sums = jnp.dot(jnp.concatenate([t[2] for t in mxu_terms], axis=0).astype(BF16), ones_kk,
                           preferred_element_type=F32)
            for n, (p, vs, _) in enumerate(mxu_terms):
                o_h[p] = o_h[p] + sums[n * half:(n + 1) * half] * vs
            o = jnp.concatenate(o_h, axis=0)
            kd = (k * jnp.exp(b_last - b)).astype(BF16)
            upd = lax.dot_general(v.astype(BF16), kd, (((0,), (0,)), ((), ())),
                                  preferred_element_type=F32)
            st_scr[h] = st * jnp.exp(b_last) + upd
            o = o * lax.rsqrt(jnp.mean(o * o, axis=-1, keepdims=True) + RMS_EPS)
            outs.append(o * gn_ref[:, sl] * sg4[:, sl])
        o_ref[i] = jnp.concatenate(outs, axis=1).astype(o_ref.dtype)
        return carry

    lax.fori_loop(0, tb // c, chunk, 0, unroll=min(2, tb // c))

    @pl.when(t_blk == pl.num_programs(1) - 1)
    def _():
        for h in range(H_A):
            sout_ref[0, h] = st_scr[h].T


def _hgrn(hq, hk, lf, hv, sg, gn, s0, *, bsz, t_len, tb):
    nt = t_len // tb
    c = HGRN_CHUNK
    n = bsz * t_len
    row = pl.BlockSpec((tb // c, c, D_A), lambda b, t: (b * nt + t, 0, 0))
    state = pl.BlockSpec((1, H_A, K_A, V_A), lambda b, t: (b, 0, 0, 0))
    ha, s_new = pl.pallas_call(
        functools.partial(_hgrn_kernel, tb=tb),
        grid=(bsz, nt),
        in_specs=[row, row, row, row, row, _resident(gn.shape), state],
        out_specs=[row, state],
        out_shape=[jax.ShapeDtypeStruct((n // c, c, D_AV), BF16),
                   jax.ShapeDtypeStruct((bsz, H_A, K_A, V_A), F32)],
        scratch_shapes=[pltpu.VMEM((H_A, V_A, K_A), F32), pltpu.VMEM((c, D_A), F32)],
        compiler_params=_params(("parallel", "arbitrary")),
        name="hgrn",
    )(*(a.reshape(n // c, c, D_A) for a in (hq, hk, lf, hv, sg)), gn, s0)
    return ha.reshape(n, D_AV), s_new


def _shift_up(x, sh):
    row = lax.broadcasted_iota(jnp.int32, x.shape, 0)
    return jnp.where(row < V7X_SUBLANES - sh, pltpu.roll(x, V7X_SUBLANES - sh, axis=0), 1.0)


def _sb_weights(z_ref, w_ref, carry, *, q_off=None, scale=None):
    tq = z_ref.shape[1]
    new_carry = []
    for c0 in range(0, tq, V7X_LANES):
        cols = slice(c0, c0 + V7X_LANES)
        if q_off is not None:
            sub = lax.broadcasted_iota(jnp.int32, (V7X_SUBLANES, V7X_LANES), 0) * SB_NV
            lane = lax.broadcasted_iota(jnp.int32, (V7X_SUBLANES, V7X_LANES), 1) + (q_off + c0)
        run = jnp.ones((V7X_SUBLANES, V7X_LANES), F32)
        diffs = [None] * SB_NV
        for v in reversed(range(SB_NV)):
            gv = 0.5 - 0.5 * jnp.tanh(z_ref[v * V7X_SUBLANES:(v + 1) * V7X_SUBLANES, cols])
            if q_off is not None:
                gv = jnp.where(sub + v < lane, gv, 1.0)
            nxt = run * gv
            diffs[v] = run - nxt
            run = nxt
        y = _shift_up(run, 1)
        y = y * _shift_up(y, 1)
        y = y * _shift_up(y, 2)
        y = y * _shift_up(y, 4)
        offs = carry[:, cols] * y
        if scale is not None:
            offs = offs * scale
        for v in range(0, SB_NV, 2):
            pair = jnp.concatenate([diffs[v] * offs, diffs[v + 1] * offs], axis=0)
            w_ref[v * V7X_SUBLANES:(v + 2) * V7X_SUBLANES, cols] = pair.astype(BF16)
        new_carry.append(jnp.broadcast_to((offs * run)[0:1, :], run.shape))
    return jnp.concatenate(new_carry, axis=1)


def _sb_kernel(qt_ref, k_ref, vt_ref, o_ref, z_scr, w_scr, acc_scr, *, tq, q_start):
    q0 = q_start + pl.program_id(2) * tq
    n_full = q0 // SB_TK
    n_pairs = (n_full + 1) // 2
    qt2 = qt_ref[0]
    half = lax.broadcasted_iota(jnp.int32, qt2.shape, 0) < D_HB
    zero = jnp.zeros_like(qt2)
    qts = (jnp.where(half, qt2, zero), jnp.where(half, zero, qt2))

    def scores(j, slot):
        s0 = pl.multiple_of(jnp.maximum(j, 0) * SB_TK, SB_TK)
        kb = k_ref[0, pl.ds(s0, SB_TK), :]
        for a in range(2):
            z_scr[slot, a] = jnp.dot(kb, qts[a], preferred_element_type=F32)

    def values(j, slot):
        s0 = pl.multiple_of(jnp.maximum(j, 0) * SB_TK, SB_TK)
        parts = [jnp.dot(vt_ref[0, a * D_HB:(a + 1) * D_HB, pl.ds(s0, SB_TK)], w_scr[slot, a],
                         preferred_element_type=F32) for a in range(2)]
        acc_scr[...] += jnp.concatenate(parts, axis=0)

    def weights(slot, carries, **kw):
        return tuple(_sb_weights(z_scr.at[slot, a], w_scr.at[slot, a], carries[a], **kw) for a in range(2))

    scores(n_full, 1)
    scores(n_full - 1, 0)
    acc_scr[...] = jnp.zeros_like(acc_scr)
    ones = jnp.ones((V7X_SUBLANES, tq), F32)
    carries = weights(1, (ones, ones), q_off=q0 - n_full * SB_TK)

    def pair(state):
        p, carries, _ = state
        j = n_full - 1 - 2 * p
        values(j + 1, 1)
        scores(j - 1, 1)
        carries = weights(0, carries)
        alive = (jnp.max(jnp.maximum(carries[0], carries[1])) > 0.0).astype(jnp.int32)
        scores(j - 2, 0)
        values(j, 0)
        valid = (j >= 1).astype(F32)
        return p + 1, weights(1, carries, scale=valid), alive

    def more(state):
        return jnp.logical_and(state[0] < n_pairs, state[2] > 0)

    n_done, _, _ = lax.while_loop(more, pair, (jnp.int32(0), carries, jnp.int32(1)))
    values(n_full - 2 * n_done, 1)
    o_ref[...] = acc_scr[...].T.astype(o_ref.dtype)


def _sb(qt, k_perm, vt_perm, *, tq, q_start):
    bsz, _, t_q = qt.shape
    s_len = k_perm.shape[1]
    nq = t_q // tq
    assert tq <= SB_TK and q_start % SB_TK == 0 and SB_TK % tq == 0
    assert s_len % SB_TK == 0 and s_len >= q_start + nq * tq - tq + SB_TK
    pair = 2 * D_HB
    return pl.pallas_call(
        functools.partial(_sb_kernel, tq=tq, q_start=q_start),
        grid=(bsz, H_B // 2, nq),
        in_specs=[pl.BlockSpec((1, pair, tq), lambda b, h, i: (b, h, i)),
                  pl.BlockSpec((1, s_len, pair), lambda b, h, i: (b, 0, h)),
                  pl.BlockSpec((1, pair, s_len), lambda b, h, i: (b, h, 0))],
        out_specs=pl.BlockSpec((tq, pair), lambda b, h, i: (b * nq + i, h)),
        out_shape=jax.ShapeDtypeStruct((bsz * t_q, D_B), BF16),
        scratch_shapes=[pltpu.VMEM((2, 2, SB_TK, tq), F32), pltpu.VMEM((2, 2, SB_TK, tq), BF16),
                        pltpu.VMEM((pair, tq), F32)],
        compiler_params=_params(("parallel", "parallel", "arbitrary")),
        name="sb",
    )(qt, k_perm, vt_perm)


def _sb_prep_kernel(ck_ref, cv_ref, nk_ref, nv_ref, kp_ref, vt_ref, *, n_past):
    j = pl.program_id(1)

    def emit(k, v):
        kp_ref[0] = jnp.dot(_key_permutation(False), k.astype(BF16), preferred_element_type=F32).astype(BF16)
        vt_ref[0] = jnp.dot(v.T.astype(BF16), _key_permutation(True), preferred_element_type=F32).astype(BF16)

    @pl.when(j < n_past)
    def _():
        emit(ck_ref[0], cv_ref[0])

    @pl.when(j == n_past)
    def _():
        emit(nk_ref[0], nv_ref[0])


def _sb_prep(cache_k, cache_v, new_k, new_v):
    bsz, p_len, _ = cache_k.shape
    assert p_len % SB_TK == 0
    n_past = p_len // SB_TK
    past = pl.BlockSpec((1, SB_TK, D_B), lambda b, j: (b, jnp.minimum(j, n_past - 1), 0))
    new = pl.BlockSpec((1, SB_TK, D_B), lambda b, j: (b, 0, 0))
    return pl.pallas_call(
        functools.partial(_sb_prep_kernel, n_past=n_past),
        grid=(bsz, n_past + 1),
        in_specs=[past, past, new, new],
        out_specs=[pl.BlockSpec((1, SB_TK, D_B), lambda b, j: (b, j, 0)),
                   pl.BlockSpec((1, D_B, SB_TK), lambda b, j: (b, 0, j))],
        out_shape=[jax.ShapeDtypeStruct((bsz, p_len + SB_TK, D_B), BF16),
                   jax.ShapeDtypeStruct((bsz, D_B, p_len + SB_TK), BF16)],
        compiler_params=_params(("parallel", "arbitrary")),
        name="sb_prep",
    )(cache_k, cache_v, new_k, new_v)


def _mix_ffn_kernel(x_ref, ha_ref, hb_ref, ga_ref, gb_ref, wa_ref, wb_ref, wo_ref, g2_ref, b2_ref,
                    wg_ref, wu_ref, wd_ref, g3_ref, b3_ref, o_ref, *, alpha):
    pa = jnp.dot(ha_ref[...], wa_ref[...], preferred_element_type=F32)
    pb = jnp.dot(hb_ref[...], wb_ref[...], preferred_element_type=F32)
    merged = ga_ref[...] * pa + gb_ref[...] * pb
    mix = jnp.dot(merged.astype(BF16), wo_ref[...], preferred_element_type=F32)
    x2 = _layer_norm(alpha * x_ref[...] + mix, g2_ref[...], b2_ref[...])
    ff = _swiglu(x2.astype(BF16), wg_ref, wu_ref, wd_ref)
    o_ref[...] = _layer_norm(alpha * x2 + 0.5 * ff, g3_ref[...], b3_ref[...])


def _mix_ffn(x1, ha, hb, ga, gb, wa, wb, wo, g2, b2, wg, wu, wd, g3, b3, *, alpha, tm):
    n = x1.shape[0]

    def row(width):
        return pl.BlockSpec((tm, width), lambda i: (i, 0))

    consts = (wa, wb, wo, g2, b2, wg, wu, wd, g3, b3)
    return pl.pallas_call(
        functools.partial(_mix_ffn_kernel, alpha=alpha),
        grid=(n // tm,),
        in_specs=[row(D_MODEL), row(D_AV), row(D_B), row(D_MODEL), row(D_MODEL)]
                 + [_resident(c.shape) for c in consts],
        out_specs=row(D_MODEL),
        out_shape=jax.ShapeDtypeStruct((n, D_MODEL), F32),
        compiler_params=_params(("parallel",)),
        name="mix_ffn",
    )(x1, ha, hb, ga, gb, *consts)


def _pick_tile(n, cap):
    t = min(n, cap)
    assert n % t == 0
    return t


def _trunk(x, s0, past_k, past_v, w):
    bsz, t_len, _ = x.shape
    n = bsz * t_len
    depth = w["w_in"].shape[0]
    alpha = (2 * depth) ** 0.25
    tm = _pick_tile(n, 512)
    xf = x.reshape(n, D_MODEL)
    ks, vs, ss = [], [], []
    for l in range(depth):
        x1 = _ffn_ln(xf, w["ffn1_wg"][l], w["ffn1_wu"][l], w["ffn1_wd"][l], w["ln1_g"][l], w["ln1_b"][l],
                     alpha=alpha, tm=tm)
        sb_layout = past_k is None
        hq, hk, lf, hv, sg, kb, vb, ga, gb, *sb_ops = _inproj(
            x1, w["w_in"][l], w["lb_logits"], layer=l, tm=tm, t_len=t_len, sb_layout=sb_layout)
        ha, s_new = _hgrn(hq, hk, lf, hv, sg, w["hgrn_norm_g"][l], s0[l],
                          bsz=bsz, t_len=t_len, tb=_pick_tile(t_len, 512))

        if sb_layout:
            qt, k_perm, vt_perm = sb_ops
            hb = _sb(qt, k_perm.reshape(bsz, t_len, D_B), vt_perm, tq=SB_TK, q_start=0)
        else:
            q_start = past_k.shape[2]
            tq = V7X_LANES
            assert t_len <= tq
            pad_new = ((0, 0), (0, SB_TK - t_len), (0, 0))
            k_perm, vt_perm = _sb_prep(past_k[l].reshape(bsz, q_start, D_B), past_v[l].reshape(bsz, q_start, D_B),
                                       jnp.pad(kb.reshape(bsz, t_len, D_B), pad_new),
                                       jnp.pad(vb.reshape(bsz, t_len, D_B), pad_new))
            q3 = jnp.pad(sb_ops[0].reshape(bsz, t_len, D_B), ((0, 0), (0, tq - t_len), (0, 0)))
            hb = _sb(jnp.swapaxes(q3, 1, 2), k_perm, vt_perm, tq=tq, q_start=q_start)
            hb = hb.reshape(bsz, tq, D_B)[:, :t_len].reshape(n, D_B)

        xf = _mix_ffn(x1, ha, hb, ga, gb, w["w_branch_a"][l], w["w_branch_b"][l], w["w_out"][l],
                      w["ln2_g"][l], w["ln2_b"][l], w["ffn2_wg"][l], w["ffn2_wu"][l], w["ffn2_wd"][l],
                      w["ln3_g"][l], w["ln3_b"][l], alpha=alpha, tm=tm)
        ks.append(kb.reshape(bsz, t_len, H_B, D_HB))
        vs.append(vb.reshape(bsz, t_len, H_B, D_HB))
        ss.append(s_new)
    return xf.reshape(bsz, t_len, D_MODEL), jnp.stack(ks), jnp.stack(vs), jnp.stack(ss)


def kernel(x_prompt, x_sample, cache_sb_k, cache_sb_v, state_hgrn, ffn1_wg, ffn1_wu, ffn1_wd, ln1_g, ln1_b,
           w_in, lb_logits, hgrn_norm_g, w_branch_a, w_branch_b, w_out, ln2_g, ln2_b,
           ffn2_wg, ffn2_wu, ffn2_wd, ln3_g, ln3_b):
    depth = w_in.shape[0]

    def vec(p):
        return p[:, None, :]

    w = dict(
        ffn1_wg=ffn1_wg.astype(BF16), ffn1_wu=ffn1_wu.astype(BF16), ffn1_wd=ffn1_wd.astype(BF16),
        ln1_g=vec(ln1_g), ln1_b=vec(ln1_b), w_in=w_in.astype(BF16), lb_logits=lb_logits,
        hgrn_norm_g=vec(hgrn_norm_g), w_branch_a=w_branch_a.astype(BF16), w_branch_b=w_branch_b.astype(BF16),
        w_out=w_out.astype(BF16), ln2_g=vec(ln2_g), ln2_b=vec(ln2_b),
        ffn2_wg=ffn2_wg.astype(BF16), ffn2_wu=ffn2_wu.astype(BF16), ffn2_wd=ffn2_wd.astype(BF16),
        ln3_g=vec(ln3_g), ln3_b=vec(ln3_b))
    s0_prompt = jnp.zeros((depth, x_prompt.shape[0], H_A, K_A, V_A), F32)
    y_p, k_p, v_p, s_p = _trunk(x_prompt, s0_prompt, None, None, w)
    y_s, k_s, v_s, s_s = _trunk(x_sample, state_hgrn, cache_sb_k, cache_sb_v, w)
    return (y_p, y_s, k_p, v_p, s_p, k_s, v_s, s_s)
```

```python
import functools

import jax
import jax.numpy as jnp
from jax import lax
from jax.experimental import pallas as pl
from jax.experimental.pallas import tpu as pltpu

F32 = jnp.float32
BF16 = jnp.bfloat16

D_MODEL = 1024
H_A, K_A, V_A = 4, 128, 128
D_A = H_A * K_A
D_AV = H_A * V_A
H_B, D_HB = 8, 64
D_B = H_B * D_HB
D_FF = 2816
N_IN = 2 * D_A + 2 * D_AV + 3 * D_B + 2 * D_MODEL
LN_EPS = 1e-5
RMS_EPS = 1e-6
SB_SCALE = D_HB ** -0.5

V7X_LANES = 128
V7X_SUBLANES = 8
V7X_MXU_DIM = 256
V7X_VMEM_LIMIT_BYTES = 56 * 1024 * 1024

FF_CHUNK = V7X_MXU_DIM
HGRN_CHUNK = 16
HGRN_BLOCK = 64
SB_TK = 256
SB_NV = SB_TK // V7X_SUBLANES
SB_PREP_ROWS = 4 * SB_TK


def _params(sem):
    return pltpu.CompilerParams(dimension_semantics=sem, vmem_limit_bytes=V7X_VMEM_LIMIT_BYTES)


def _resident(shape):
    nd = len(shape)
    return pl.BlockSpec(shape, lambda *_: (0,) * nd, pipeline_mode=pl.Buffered(1))


def _layer_norm(y, g, b):
    mu = jnp.mean(y, axis=-1, keepdims=True)
    d = y - mu
    var = jnp.mean(d * d, axis=-1, keepdims=True)
    return d * lax.rsqrt(var + LN_EPS) * g + b


def _silu(x):
    return x * jax.nn.sigmoid(x)


def _swiglu(xb, wg_ref, wu_ref, wd_ref):
    acc = jnp.zeros((xb.shape[0], D_MODEL), F32)
    for c in range(D_FF // FF_CHUNK):
        sl = slice(c * FF_CHUNK, (c + 1) * FF_CHUNK)
        g = jnp.dot(xb, wg_ref[:, sl], preferred_element_type=F32)
        u = jnp.dot(xb, wu_ref[:, sl], preferred_element_type=F32)
        h = (_silu(g) * u).astype(BF16)
        acc = acc + jnp.dot(h, wd_ref[sl, :], preferred_element_type=F32)
    return acc


def _ffn_ln_kernel(x_ref, wg_ref, wu_ref, wd_ref, g_ref, b_ref, o_ref, *, alpha):
    x = x_ref[...]
    ff = _swiglu(x.astype(BF16), wg_ref, wu_ref, wd_ref)
    o_ref[...] = _layer_norm(alpha * x + 0.5 * ff, g_ref[...], b_ref[...])


def _ffn_ln(x, wg, wu, wd, g, b, *, alpha, tm):
    n = x.shape[0]
    row = pl.BlockSpec((tm, D_MODEL), lambda i: (i, 0))
    return pl.pallas_call(
        functools.partial(_ffn_ln_kernel, alpha=alpha),
        grid=(n // tm,),
        in_specs=[row, _resident(wg.shape), _resident(wu.shape), _resident(wd.shape),
                  _resident(g.shape), _resident(b.shape)],
        out_specs=row,
        out_shape=jax.ShapeDtypeStruct((n, D_MODEL), F32),
        compiler_params=_params(("parallel",)),
        name="ffn_ln",
    )(x, wg, wu, wd, g, b)


_CUTS = (0, D_A, 2 * D_A, 2 * D_A + D_AV, 2 * D_A + 2 * D_AV,
         2 * D_A + 2 * D_AV + D_B, 2 * D_A + 2 * D_AV + 2 * D_B,
         2 * D_A + 2 * D_AV + 3 * D_B, 2 * D_A + 2 * D_AV + 3 * D_B + D_MODEL, N_IN)


def _key_permutation(transposed):
    i = lax.broadcasted_iota(jnp.int32, (SB_TK, SB_TK), 1 if transposed else 0)
    j = lax.broadcasted_iota(jnp.int32, (SB_TK, SB_TK), 0 if transposed else 1)
    src = (i % V7X_SUBLANES) * SB_NV + i // V7X_SUBLANES
    return jnp.where(j == src, 1.0, 0.0).astype(BF16)


def _inproj_kernel(x_ref, w_ref, lbl_ref, hq_ref, hk_ref, lf_ref, hv_ref, sg_ref,
                   kb_ref, vb_ref, ga_ref, gb_ref, *sb_refs, layer, sb_layout):
    xb = x_ref[...].astype(BF16)

    def proj(i):
        return jnp.dot(xb, w_ref[:, _CUTS[i]:_CUTS[i + 1]], preferred_element_type=F32)

    hq_ref[...] = _silu(proj(0))
    rows = [lbl_ref[i:i + 1, :] for i in range(lbl_ref.shape[0])]
    m = functools.reduce(jnp.maximum, rows)
    ex = [jnp.exp(r - m) for r in rows]
    one_minus_lb = sum(ex[layer + 1:]) / sum(ex)
    hk = one_minus_lb * jax.nn.sigmoid(-proj(1))
    hk_ref[...] = hk
    lf_ref[...] = jnp.log1p(-hk)
    hv_ref[...] = proj(2)
    sg_ref[...] = _silu(proj(3))
    ga_ref[...] = jax.nn.sigmoid(proj(7))
    gb_ref[...] = jax.nn.sigmoid(proj(8))
    qh = proj(4) * (0.5 * SB_SCALE)
    kf = proj(5)
    vf = proj(6)
    kb_ref[...] = kf
    vb_ref[...] = vf
    if not sb_layout:
        sb_refs[0][...] = qh.astype(BF16)
        return
    qt_ref, kp_ref, vt_ref = sb_refs
    qt_ref[0] = qh.T.astype(BF16)
    perm = _key_permutation(False)
    perm_t = _key_permutation(True)
    for blk in range(x_ref.shape[0] // SB_TK):
        rows = slice(blk * SB_TK, (blk + 1) * SB_TK)
        kp_ref[rows, :] = jnp.dot(perm, kf[rows].astype(BF16), preferred_element_type=F32).astype(BF16)
        vt_ref[0, :, rows] = jnp.dot(vf[rows].T.astype(BF16), perm_t, preferred_element_type=F32).astype(BF16)


def _inproj(x1, w_in, lb_logits, *, layer, tm, t_len, sb_layout):
    n = x1.shape[0]
    nt = t_len // tm

    def row(width):
        return pl.BlockSpec((tm, width), lambda i: (i, 0))

    cols = pl.BlockSpec((1, D_B, tm), lambda i: (i // nt, 0, i % nt))
    widths = (D_A, D_A, D_A, D_AV, D_AV, D_B, D_B, D_MODEL, D_MODEL)
    specs = [row(w) for w in widths]
    shapes = [jax.ShapeDtypeStruct((n, w), F32) for w in widths]
    if sb_layout:
        assert tm % SB_TK == 0 and t_len % tm == 0
        specs += [cols, row(D_B), cols]
        shapes += [jax.ShapeDtypeStruct((n // t_len, D_B, t_len), BF16), jax.ShapeDtypeStruct((n, D_B), BF16),
                   jax.ShapeDtypeStruct((n // t_len, D_B, t_len), BF16)]
    else:
        specs += [row(D_B)]
        shapes += [jax.ShapeDtypeStruct((n, D_B), BF16)]
    return pl.pallas_call(
        functools.partial(_inproj_kernel, layer=layer, sb_layout=sb_layout),
        grid=(n // tm,),
        in_specs=[row(D_MODEL), _resident(w_in.shape), _resident(lb_logits.shape)],
        out_specs=specs,
        out_shape=shapes,
        compiler_params=_params(("parallel",)),
        name="inproj",
    )(x1, w_in, lb_logits)


def _cumsum_rows(x):
    n = x.shape[0]
    row = lax.broadcasted_iota(jnp.int32, x.shape, 0)
    sh = 1
    while sh < n:
        x = x + jnp.where(row >= sh, pltpu.roll(x, sh, axis=0), 0.0)
        sh *= 2
    return x


def _hgrn_kernel(q_ref, k_ref, lf_ref, v_ref, sg_ref, gn_ref, s0_ref, o_ref, sout_ref, st_scr, b_scr, *, tb):
    c = HGRN_CHUNK
    blk = q_ref.shape[1]
    ns = blk // c
    t_blk = pl.program_id(1)

    @pl.when(t_blk == 0)
    def _():
        for h in range(H_A):
            st_scr[h] = s0_ref[0, h].T

    half = V7X_SUBLANES
    row = lax.broadcasted_iota(jnp.int32, (half, K_A), 0)
    ones_kk = jnp.ones((K_A, K_A), BF16)
    nt_dims = (((1,), (1,)), ((), ()))

    def block(i, carry):
        q4 = q_ref[i]
        k4 = k_ref[i]
        v4 = v_ref[i]
        sg4 = sg_ref[i]
        lf4 = lf_ref[i]
        b4 = jnp.concatenate([_cumsum_rows(lf4[j * c:(j + 1) * c]) for j in range(ns)], axis=0)
        b_scr[...] = b4
        outs = [None] * H_A

        def head(h):
            sl = slice(h * K_A, (h + 1) * K_A)
            sub = [slice(j * c, (j + 1) * c) for j in range(ns)]
            q = [q4[r, sl] for r in sub]
            k = [k4[r, sl] for r in sub]
            v = [v4[r, sl] for r in sub]
            b = [b4[r, sl] for r in sub]
            e = [x[c - 1:c, :] for x in b]

            def span(lo, hi):
                return functools.reduce(jnp.add, e[lo:hi]) if hi > lo else None

            def decayed(x, bb, extra):
                return x * jnp.exp(bb if extra is None else bb + extra)

            st = st_scr[h]
            qs = jnp.concatenate([decayed(q[n], b[n], span(0, n)) for n in range(ns)], axis=0)
            o_all = lax.dot_general(qs.astype(BF16), st.astype(BF16), nt_dims, preferred_element_type=F32)
            o = [o_all[r] for r in sub]
            kh = [k[j] * jnp.exp(e[j] - b[j]) for j in range(ns)]
            sc = []
            for j in range(ns - 1):
                qx = jnp.concatenate([decayed(q[n], b[n], span(j + 1, n)) for n in range(j + 1, ns)], axis=0)
                sc.append(lax.dot_general(qx.astype(BF16), kh[j].astype(BF16), nt_dims,
                                          preferred_element_type=F32))
            yield
            for j in range(ns - 1):
                ov = jnp.dot(sc[j].astype(BF16), v[j].astype(BF16), preferred_element_type=F32)
                for m, n in enumerate(range(j + 1, ns)):
                    o[n] = o[n] + ov[m * c:(m + 1) * c]
            yield
            o_h = [[x[:half], x[half:]] for x in o]
            mxu_terms = []
            for j in range(ns):
                for s in range(c):
                    ks = k_ref[i, j * c + s:j * c + s + 1, sl]
                    vs = v_ref[i, j * c + s:j * c + s + 1, sl]
                    bs = b_scr[j * c + s:j * c + s + 1, sl]
                    for p in range(2):
                        lo = p * half
                        if s >= lo + half:
                            continue
                        w = q[j][lo:lo + half] * jnp.exp(b[j][lo:lo + half] - bs) * ks
                        if s > lo:
                            w = jnp.where(row >= s - lo, w, 0.0)
                        if (s + p) % 2:
                            mxu_terms.append((j, p, vs, w))
                        else:
                            o_h[j][p] = o_h[j][p] + jnp.sum(w, axis=-1, keepdims=True) * vs
            sums = jnp.dot(jnp.concatenate([t[3] for t in mxu_terms], axis=0).astype(BF16), ones_kk,
                           preferred_element_type=F32)
            for n, (j, p, vs, _) in enumerate(mxu_terms):
                o_h[j][p] = o_h[j][p] + sums[n * half:(n + 1) * half] * vs
            o = jnp.concatenate([x for pair in o_h for x in pair], axis=0)
            yield
            kd = jnp.concatenate([kh[j] if j == ns - 1 else kh[j] * jnp.exp(span(j + 1, ns))
                                  for j in range(ns)], axis=0)
            vv = jnp.concatenate(v, axis=0)
            upd = lax.dot_general(vv.astype(BF16), kd.astype(BF16), (((0,), (0,)), ((), ())),
                                  preferred_element_type=F32)
            st_scr[h] = st * jnp.exp(span(0, ns)) + upd
            o = o * lax.rsqrt(jnp.mean(o * o, axis=-1, keepdims=True) + RMS_EPS)
            outs[h] = o * gn_ref[:, sl] * sg4[:, sl]
            yield

        heads = [head(h) for h in range(H_A)]
        for _ in range(4):
            for g in heads:
                next(g)
        o_ref[i] = jnp.concatenate(outs, axis=1).astype(o_ref.dtype)
        return carry

    lax.fori_loop(0, tb // blk, block, 0)

    @pl.when(t_blk == pl.num_programs(1) - 1)
    def _():
        for h in range(H_A):
            sout_ref[0, h] = st_scr[h].T


def _hgrn(hq, hk, lf, hv, sg, gn, s0, *, bsz, t_len, tb):
    nt = t_len // tb
    c = min(HGRN_BLOCK, tb)
    assert tb % c == 0 and c % HGRN_CHUNK == 0
    n = bsz * t_len
    row = pl.BlockSpec((tb // c, c, D_A), lambda b, t: (b * nt + t, 0, 0))
    state = pl.BlockSpec((1, H_A, K_A, V_A), lambda b, t: (b, 0, 0, 0))
    ha, s_new = pl.pallas_call(
        functools.partial(_hgrn_kernel, tb=tb),
        grid=(bsz, nt),
        in_specs=[row, row, row, row, row, _resident(gn.shape), state],
        out_specs=[row, state],
        out_shape=[jax.ShapeDtypeStruct((n // c, c, D_AV), BF16),
                   jax.ShapeDtypeStruct((bsz, H_A, K_A, V_A), F32)],
        scratch_shapes=[pltpu.VMEM((H_A, V_A, K_A), F32), pltpu.VMEM((c, D_A), F32)],
        compiler_params=_params(("parallel", "arbitrary")),
        name="hgrn",
    )(*(a.reshape(n // c, c, D_A) for a in (hq, hk, lf, hv, sg)), gn, s0)
    return ha.reshape(n, D_AV), s_new


def _shift_up(x, sh):
    row = lax.broadcasted_iota(jnp.int32, x.shape, 0)
    return jnp.where(row < V7X_SUBLANES - sh, pltpu.roll(x, V7X_SUBLANES - sh, axis=0), 1.0)


def _sb_weights(z_ref, w_ref, carry, *, q_off=None, scale=None):
    tq = z_ref.shape[1]
    new_carry = []
    for c0 in range(0, tq, V7X_LANES):
        cols = slice(c0, c0 + V7X_LANES)
        if q_off is not None:
            sub = lax.broadcasted_iota(jnp.int32, (V7X_SUBLANES, V7X_LANES), 0) * SB_NV
            lane = lax.broadcasted_iota(jnp.int32, (V7X_SUBLANES, V7X_LANES), 1) + (q_off + c0)
        run = jnp.ones((V7X_SUBLANES, V7X_LANES), F32)
        diffs = [None] * SB_NV
        for v in reversed(range(SB_NV)):
            gv = 0.5 - 0.5 * jnp.tanh(z_ref[v * V7X_SUBLANES:(v + 1) * V7X_SUBLANES, cols])
            if q_off is not None:
                gv = jnp.where(sub + v < lane, gv, 1.0)
            nxt = run * gv
            diffs[v] = run - nxt
            run = nxt
        y = _shift_up(run, 1)
        y = y * _shift_up(y, 1)
        y = y * _shift_up(y, 2)
        y = y * _shift_up(y, 4)
        offs = carry[:, cols] * y
        if scale is not None:
            offs = offs * scale
        for v in range(0, SB_NV, 2):
            pair = jnp.concatenate([diffs[v] * offs, diffs[v + 1] * offs], axis=0)
            w_ref[v * V7X_SUBLANES:(v + 2) * V7X_SUBLANES, cols] = pair.astype(BF16)
        new_carry.append(jnp.broadcast_to((offs * run)[0:1, :], run.shape))
    return jnp.concatenate(new_carry, axis=1)


def _sb_kernel(qt_ref, k_ref, vt_ref, o_ref, z_scr, w_scr, acc_scr, *, tq, q_start):
    q0 = q_start + pl.program_id(2) * tq
    n_full = q0 // SB_TK
    n_pairs = (n_full + 1) // 2
    qt2 = qt_ref[0]
    half = lax.broadcasted_iota(jnp.int32, qt2.shape, 0) < D_HB
    zero = jnp.zeros_like(qt2)
    qts = (jnp.where(half, qt2, zero), jnp.where(half, zero, qt2))

    def scores(j, slot):
        s0 = pl.multiple_of(jnp.maximum(j, 0) * SB_TK, SB_TK)
        kb = k_ref[0, pl.ds(s0, SB_TK), :]
        for a in range(2):
            z_scr[slot, a] = jnp.dot(kb, qts[a], preferred_element_type=F32)

    def values(j, slot):
        s0 = pl.multiple_of(jnp.maximum(j, 0) * SB_TK, SB_TK)
        parts = [jnp.dot(vt_ref[0, a * D_HB:(a + 1) * D_HB, pl.ds(s0, SB_TK)], w_scr[slot, a],
                         preferred_element_type=F32) for a in range(2)]
        acc_scr[...] += jnp.concatenate(parts, axis=0)

    def weights(slot, carries, **kw):
        return tuple(_sb_weights(z_scr.at[slot, a], w_scr.at[slot, a], carries[a], **kw) for a in range(2))

    scores(n_full, 1)
    scores(n_full - 1, 0)
    acc_scr[...] = jnp.zeros_like(acc_scr)
    ones = jnp.ones((V7X_SUBLANES, tq), F32)
    carries = weights(1, (ones, ones), q_off=q0 - n_full * SB_TK)

    def pair(state):
        p, carries, _ = state
        j = n_full - 1 - 2 * p
        values(j + 1, 1)
        scores(j - 1, 1)
        carries = weights(0, carries)
        alive = (jnp.max(jnp.maximum(carries[0], carries[1])) > 0.0).astype(jnp.int32)
        scores(j - 2, 0)
        values(j, 0)
        valid = (j >= 1).astype(F32)
        return p + 1, weights(1, carries, scale=valid), alive

    def more(state):
        return jnp.logical_and(state[0] < n_pairs, state[2] > 0)

    n_done, _, _ = lax.while_loop(more, pair, (jnp.int32(0), carries, jnp.int32(1)))
    values(n_full - 2 * n_done, 1)
    o_ref[...] = acc_scr[...].T.astype(o_ref.dtype)


def _sb(qt, k_perm, vt_perm, *, tq, q_start):
    bsz, _, t_q = qt.shape
    s_len = k_perm.shape[1]
    nq = t_q // tq
    assert tq <= SB_TK and q_start % SB_TK == 0 and SB_TK % tq == 0
    assert s_len % SB_TK == 0 and s_len >= q_start + nq * tq - tq + SB_TK
    pair = 2 * D_HB
    return pl.pallas_call(
        functools.partial(_sb_kernel, tq=tq, q_start=q_start),
        grid=(bsz, H_B // 2, nq),
        in_specs=[pl.BlockSpec((1, pair, tq), lambda b, h, i: (b, h, i)),
                  pl.BlockSpec((1, s_len, pair), lambda b, h, i: (b, 0, h)),
                  pl.BlockSpec((1, pair, s_len), lambda b, h, i: (b, h, 0))],
        out_specs=pl.BlockSpec((tq, pair), lambda b, h, i: (b * nq + i, h)),
        out_shape=jax.ShapeDtypeStruct((bsz * t_q, D_B), BF16),
        scratch_shapes=[pltpu.VMEM((2, 2, SB_TK, tq), F32), pltpu.VMEM((2, 2, SB_TK, tq), BF16),
                        pltpu.VMEM((pair, tq), F32)],
        compiler_params=_params(("parallel", "parallel", "arbitrary")),
        name="sb",
    )(qt, k_perm, vt_perm)


def _sb_prep_kernel(ck_ref, cv_ref, nk_ref, nv_ref, kp_ref, vt_ref, *, n_past):
    j = pl.program_id(1)
    perm = _key_permutation(False)
    perm_t = _key_permutation(True)

    def emit(k_ref, v_ref):
        for blk in range(k_ref.shape[1] // SB_TK):
            rows = slice(blk * SB_TK, (blk + 1) * SB_TK)
            kp_ref[0, rows, :] = jnp.dot(perm, k_ref[0, rows, :].astype(BF16),
                                         preferred_element_type=F32).astype(BF16)
            vt_ref[0, :, rows] = jnp.dot(v_ref[0, rows, :].astype(F32).T.astype(BF16), perm_t,
                                         preferred_element_type=F32).astype(BF16)

    @pl.when(j < n_past)
    def _():
        emit(ck_ref, cv_ref)

    @pl.when(j == n_past)
    def _():
        emit(nk_ref, nv_ref)
        kp_ref[0, SB_TK:, :] = jnp.zeros((SB_PREP_ROWS - SB_TK, D_B), BF16)
        vt_ref[0, :, SB_TK:] = jnp.zeros((D_B, SB_PREP_ROWS - SB_TK), BF16)


def _sb_prep(cache_k, cache_v, new_k, new_v):
    bsz, p_len, _ = cache_k.shape
    assert p_len % SB_PREP_ROWS == 0
    n_past = p_len // SB_PREP_ROWS
    past = pl.BlockSpec((1, SB_PREP_ROWS, D_B), lambda b, j: (b, jnp.minimum(j, n_past - 1), 0))
    new = pl.BlockSpec((1, SB_TK, D_B), lambda b, j: (b, 0, 0))
    return pl.pallas_call(
        functools.partial(_sb_prep_kernel, n_past=n_past),
        grid=(bsz, n_past + 1),
        in_specs=[past, past, new, new],
        out_specs=[pl.BlockSpec((1, SB_PREP_ROWS, D_B), lambda b, j: (b, j, 0)),
                   pl.BlockSpec((1, D_B, SB_PREP_ROWS), lambda b, j: (b, 0, j))],
        out_shape=[jax.ShapeDtypeStruct((bsz, p_len + SB_PREP_ROWS, D_B), BF16),
                   jax.ShapeDtypeStruct((bsz, D_B, p_len + SB_PREP_ROWS), BF16)],
        compiler_params=_params(("parallel", "arbitrary")),
        name="sb_prep",
    )(cache_k, cache_v, new_k, new_v)


def _mix_ffn_kernel(x_ref, ha_ref, hb_ref, ga_ref, gb_ref, wa_ref, wb_ref, wo_ref, g2_ref, b2_ref,
                    wg_ref, wu_ref, wd_ref, g3_ref, b3_ref, o_ref, *, alpha):
    pa = jnp.dot(ha_ref[...], wa_ref[...], preferred_element_type=F32)
    pb = jnp.dot(hb_ref[...], wb_ref[...], preferred_element_type=F32)
    merged = ga_ref[...] * pa + gb_ref[...] * pb
    mix = jnp.dot(merged.astype(BF16), wo_ref[...], preferred_element_type=F32)
    x2 = _layer_norm(alpha * x_ref[...] + mix, g2_ref[...], b2_ref[...])
    ff = _swiglu(x2.astype(BF16), wg_ref, wu_ref, wd_ref)
    o_ref[...] = _layer_norm(alpha * x2 + 0.5 * ff, g3_ref[...], b3_ref[...])


def _mix_ffn(x1, ha, hb, ga, gb, wa, wb, wo, g2, b2, wg, wu, wd, g3, b3, *, alpha, tm):
    n = x1.shape[0]

    def row(width):
        return pl.BlockSpec((tm, width), lambda i: (i, 0))

    consts = (wa, wb, wo, g2, b2, wg, wu, wd, g3, b3)
    return pl.pallas_call(
        functools.partial(_mix_ffn_kernel, alpha=alpha),
        grid=(n // tm,),
        in_specs=[row(D_MODEL), row(D_AV), row(D_B), row(D_MODEL), row(D_MODEL)]
                 + [_resident(c.shape) for c in consts],
        out_specs=row(D_MODEL),
        out_shape=jax.ShapeDtypeStruct((n, D_MODEL), F32),
        compiler_params=_params(("parallel",)),
        name="mix_ffn",
    )(x1, ha, hb, ga, gb, *consts)


def _pick_tile(n, cap):
    t = min(n, cap)
    assert n % t == 0
    return t


def _trunk(x, s0, past_k, past_v, w):
    bsz, t_len, _ = x.shape
    n = bsz * t_len
    depth = w["w_in"].shape[0]
    alpha = (2 * depth) ** 0.25
    tm = _pick_tile(n, 512)
    xf = x.reshape(n, D_MODEL)
    ks, vs, ss = [], [], []
    for l in range(depth):
        x1 = _ffn_ln(xf, w["ffn1_wg"][l], w["ffn1_wu"][l], w["ffn1_wd"][l], w["ln1_g"][l], w["ln1_b"][l],
                     alpha=alpha, tm=tm)
        sb_layout = past_k is None
        hq, hk, lf, hv, sg, kb, vb, ga, gb, *sb_ops = _inproj(
            x1, w["w_in"][l], w["lb_logits"], layer=l, tm=tm, t_len=t_len, sb_layout=sb_layout)
        ha, s_new = _hgrn(hq, hk, lf, hv, sg, w["hgrn_norm_g"][l], s0[l],
                          bsz=bsz, t_len=t_len, tb=_pick_tile(t_len, 512))

        if sb_layout:
            qt, k_perm, vt_perm = sb_ops
            hb = _sb(qt, k_perm.reshape(bsz, t_len, D_B), vt_perm, tq=SB_TK, q_start=0)
        else:
            q_start = past_k.shape[2]
            tq = V7X_LANES
            assert t_len <= tq
            pad_new = ((0, 0), (0, SB_TK - t_len), (0, 0))
            k_perm, vt_perm = _sb_prep(past_k[l].astype(BF16).reshape(bsz, q_start, D_B),
                                       past_v[l].astype(BF16).reshape(bsz, q_start, D_B),
                                       jnp.pad(kb.reshape(bsz, t_len, D_B), pad_new),
                                       jnp.pad(vb.reshape(bsz, t_len, D_B), pad_new))
            q3 = jnp.pad(sb_ops[0].reshape(bsz, t_len, D_B), ((0, 0), (0, tq - t_len), (0, 0)))
            hb = _sb(jnp.swapaxes(q3, 1, 2), k_perm, vt_perm, tq=tq, q_start=q_start)
            hb = hb.reshape(bsz, tq, D_B)[:, :t_len].reshape(n, D_B)

        xf = _mix_ffn(x1, ha, hb, ga, gb, w["w_branch_a"][l], w["w_branch_b"][l], w["w_out"][l],
                      w["ln2_g"][l], w["ln2_b"][l], w["ffn2_wg"][l], w["ffn2_wu"][l], w["ffn2_wd"][l],
                      w["ln3_g"][l], w["ln3_b"][l], alpha=alpha, tm=tm)
        ks.append(kb.reshape(bsz, t_len, H_B, D_HB))
        vs.append(vb.reshape(bsz, t_len, H_B, D_HB))
        ss.append(s_new)
    return xf.reshape(bsz, t_len, D_MODEL), jnp.stack(ks), jnp.stack(vs), jnp.stack(ss)


def kernel(x_prompt, x_sample, cache_sb_k, cache_sb_v, state_hgrn, ffn1_wg, ffn1_wu, ffn1_wd, ln1_g, ln1_b,
           w_in, lb_logits, hgrn_norm_g, w_branch_a, w_branch_b, w_out, ln2_g, ln2_b,
           ffn2_wg, ffn2_wu, ffn2_wd, ln3_g, ln3_b):
    depth = w_in.shape[0]

    def vec(p):
        return p[:, None, :]

    w = dict(
        ffn1_wg=ffn1_wg.astype(BF16), ffn1_wu=ffn1_wu.astype(BF16), ffn1_wd=ffn1_wd.astype(BF16),
        ln1_g=vec(ln1_g), ln1_b=vec(ln1_b), w_in=w_in.astype(BF16), lb_logits=lb_logits,
        hgrn_norm_g=vec(hgrn_norm_g), w_branch_a=w_branch_a.astype(BF16), w_branch_b=w_branch_b.astype(BF16),
        w_out=w_out.astype(BF16), ln2_g=vec(ln2_g), ln2_b=vec(ln2_b),
        ffn2_wg=ffn2_wg.astype(BF16), ffn2_wu=ffn2_wu.astype(BF16), ffn2_wd=ffn2_wd.astype(BF16),
        ln3_g=vec(ln3_g), ln3_b=vec(ln3_b))
    s0_prompt = jnp.zeros((depth, x_prompt.shape[0], H_A, K_A, V_A), F32)
    y_p, k_p, v_p, s_p = _trunk(x_prompt, s0_prompt, None, None, w)
    y_s, k_s, v_s, s_s = _trunk(x_sample, state_hgrn, cache_sb_k, cache_sb_v, w)
    return (y_p, y_s, k_p, v_p, s_p, k_s, v_s, s_s)
```

```python
import functools

import jax
import jax.numpy as jnp
from jax import lax
from jax.experimental import pallas as pl
from jax.experimental.pallas import tpu as pltpu

F32 = jnp.float32
BF16 = jnp.bfloat16

D_MODEL = 1024
H_A, K_A, V_A = 4, 128, 128
D_A = H_A * K_A
D_AV = H_A * V_A
H_B, D_HB = 8, 64
D_B = H_B * D_HB
D_FF = 2816
N_IN = 2 * D_A + 2 * D_AV + 3 * D_B + 2 * D_MODEL
LN_EPS = 1e-5
RMS_EPS = 1e-6
SB_SCALE = D_HB ** -0.5

V7X_LANES = 128
V7X_SUBLANES = 8
V7X_MXU_DIM = 256
V7X_VMEM_LIMIT_BYTES = 56 * 1024 * 1024

FF_CHUNK = V7X_MXU_DIM
HGRN_CHUNK = 16
HGRN_BLOCK = 64
SB_TK = 256
SB_NV = SB_TK // V7X_SUBLANES
SB_HEADS = 4
SB_PREP_ROWS = 4 * SB_TK


def _params(sem):
    return pltpu.CompilerParams(dimension_semantics=sem, vmem_limit_bytes=V7X_VMEM_LIMIT_BYTES)


def _resident(shape):
    nd = len(shape)
    return pl.BlockSpec(shape, lambda *_: (0,) * nd, pipeline_mode=pl.Buffered(1))


def _layer_norm(y, g, b):
    mu = jnp.mean(y, axis=-1, keepdims=True)
    d = y - mu
    var = jnp.mean(d * d, axis=-1, keepdims=True)
    return d * lax.rsqrt(var + LN_EPS) * g + b


def _silu(x):
    return x * jax.nn.sigmoid(x)


def _swiglu(xb, wg_ref, wu_ref, wd_ref):
    acc = jnp.zeros((xb.shape[0], D_MODEL), F32)
    for c in range(D_FF // FF_CHUNK):
        sl = slice(c * FF_CHUNK, (c + 1) * FF_CHUNK)
        g = jnp.dot(xb, wg_ref[:, sl], preferred_element_type=F32)
        u = jnp.dot(xb, wu_ref[:, sl], preferred_element_type=F32)
        h = (_silu(g) * u).astype(BF16)
        acc = acc + jnp.dot(h, wd_ref[sl, :], preferred_element_type=F32)
    return acc


def _ffn_ln_kernel(x_ref, wg_ref, wu_ref, wd_ref, g_ref, b_ref, o_ref, *, alpha):
    x = x_ref[...]
    ff = _swiglu(x.astype(BF16), wg_ref, wu_ref, wd_ref)
    o_ref[...] = _layer_norm(alpha * x + 0.5 * ff, g_ref[...], b_ref[...])


def _ffn_ln(x, wg, wu, wd, g, b, *, alpha, tm):
    n = x.shape[0]
    row = pl.BlockSpec((tm, D_MODEL), lambda i: (i, 0))
    return pl.pallas_call(
        functools.partial(_ffn_ln_kernel, alpha=alpha),
        grid=(n // tm,),
        in_specs=[row, _resident(wg.shape), _resident(wu.shape), _resident(wd.shape),
                  _resident(g.shape), _resident(b.shape)],
        out_specs=row,
        out_shape=jax.ShapeDtypeStruct((n, D_MODEL), F32),
        compiler_params=_params(("parallel",)),
        name="ffn_ln",
    )(x, wg, wu, wd, g, b)


_CUTS = (0, D_A, 2 * D_A, 2 * D_A + D_AV, 2 * D_A + 2 * D_AV,
         2 * D_A + 2 * D_AV + D_B, 2 * D_A + 2 * D_AV + 2 * D_B,
         2 * D_A + 2 * D_AV + 3 * D_B, 2 * D_A + 2 * D_AV + 3 * D_B + D_MODEL, N_IN)


def _key_permutation(transposed):
    i = lax.broadcasted_iota(jnp.int32, (SB_TK, SB_TK), 1 if transposed else 0)
    j = lax.broadcasted_iota(jnp.int32, (SB_TK, SB_TK), 0 if transposed else 1)
    src = (i % V7X_SUBLANES) * SB_NV + i // V7X_SUBLANES
    return jnp.where(j == src, 1.0, 0.0).astype(BF16)


def _inproj_kernel(x_ref, w_ref, lbl_ref, hq_ref, hk_ref, lf_ref, hv_ref, sg_ref,
                   kb_ref, vb_ref, ga_ref, gb_ref, *sb_refs, layer, sb_layout):
    xb = x_ref[...].astype(BF16)

    def proj(i):
        return jnp.dot(xb, w_ref[:, _CUTS[i]:_CUTS[i + 1]], preferred_element_type=F32)

    hq_ref[...] = _silu(proj(0))
    rows = [lbl_ref[i:i + 1, :] for i in range(lbl_ref.shape[0])]
    m = functools.reduce(jnp.maximum, rows)
    ex = [jnp.exp(r - m) for r in rows]
    one_minus_lb = sum(ex[layer + 1:]) / sum(ex)
    hk = one_minus_lb * jax.nn.sigmoid(-proj(1))
    hk_ref[...] = hk
    lf_ref[...] = jnp.log1p(-hk)
    hv_ref[...] = proj(2)
    sg_ref[...] = _silu(proj(3))
    ga_ref[...] = jax.nn.sigmoid(proj(7))
    gb_ref[...] = jax.nn.sigmoid(proj(8))
    qh = proj(4) * (0.5 * SB_SCALE)
    kf = proj(5)
    vf = proj(6)
    kb_ref[...] = kf
    vb_ref[...] = vf
    if not sb_layout:
        sb_refs[0][...] = qh.astype(BF16)
        return
    qt_ref, kp_ref, vt_ref = sb_refs
    qt_ref[0] = qh.T.astype(BF16)
    perm = _key_permutation(False)
    perm_t = _key_permutation(True)
    for blk in range(x_ref.shape[0] // SB_TK):
        rows = slice(blk * SB_TK, (blk + 1) * SB_TK)
        kp_ref[rows, :] = jnp.dot(perm, kf[rows].astype(BF16), preferred_element_type=F32).astype(BF16)
        vt_ref[0, :, rows] = jnp.dot(vf[rows].T.astype(BF16), perm_t, preferred_element_type=F32).astype(BF16)


def _inproj(x1, w_in, lb_logits, *, layer, tm, t_len, sb_layout):
    n = x1.shape[0]
    nt = t_len // tm

    def row(width):
        return pl.BlockSpec((tm, width), lambda i: (i, 0))

    cols = pl.BlockSpec((1, D_B, tm), lambda i: (i // nt, 0, i % nt))
    widths = (D_A, D_A, D_A, D_AV, D_AV, D_B, D_B, D_MODEL, D_MODEL)
    specs = [row(w) for w in widths]
    shapes = [jax.ShapeDtypeStruct((n, w), F32) for w in widths]
    if sb_layout:
        assert tm % SB_TK == 0 and t_len % tm == 0
        specs += [cols, row(D_B), cols]
        shapes += [jax.ShapeDtypeStruct((n // t_len, D_B, t_len), BF16), jax.ShapeDtypeStruct((n, D_B), BF16),
                   jax.ShapeDtypeStruct((n // t_len, D_B, t_len), BF16)]
    else:
        specs += [row(D_B)]
        shapes += [jax.ShapeDtypeStruct((n, D_B), BF16)]
    return pl.pallas_call(
        functools.partial(_inproj_kernel, layer=layer, sb_layout=sb_layout),
        grid=(n // tm,),
        in_specs=[row(D_MODEL), _resident(w_in.shape), _resident(lb_logits.shape)],
        out_specs=specs,
        out_shape=shapes,
        compiler_params=_params(("parallel",)),
        name="inproj",
    )(x1, w_in, lb_logits)


def _cumsum_rows(x):
    n = x.shape[0]
    row = lax.broadcasted_iota(jnp.int32, x.shape, 0)
    sh = 1
    while sh < n:
        x = x + jnp.where(row >= sh, pltpu.roll(x, sh, axis=0), 0.0)
        sh *= 2
    return x


def _hgrn_kernel(q_ref, k_ref, lf_ref, v_ref, sg_ref, gn_ref, s0_ref, o_ref, sout_ref, st_scr, b_scr, *, tb):
    c = HGRN_CHUNK
    blk = q_ref.shape[1]
    ns = blk // c
    t_blk = pl.program_id(1)

    @pl.when(t_blk == 0)
    def _():
        for h in range(H_A):
            st_scr[h] = s0_ref[0, h].T

    half = V7X_SUBLANES
    row = lax.broadcasted_iota(jnp.int32, (half, K_A), 0)
    ones_kk = jnp.ones((K_A, K_A), BF16)
    nt_dims = (((1,), (1,)), ((), ()))

    def block(i, carry):
        q4 = q_ref[i]
        k4 = k_ref[i]
        v4 = v_ref[i]
        sg4 = sg_ref[i]
        lf4 = lf_ref[i]
        b4 = jnp.concatenate([_cumsum_rows(lf4[j * c:(j + 1) * c]) for j in range(ns)], axis=0)
        b_scr[...] = b4
        outs = [None] * H_A

        def head(h):
            sl = slice(h * K_A, (h + 1) * K_A)
            sub = [slice(j * c, (j + 1) * c) for j in range(ns)]
            q = [q4[r, sl] for r in sub]
            k = [k4[r, sl] for r in sub]
            v = [v4[r, sl] for r in sub]
            b = [b4[r, sl] for r in sub]
            e = [x[c - 1:c, :] for x in b]

            def span(lo, hi):
                return functools.reduce(jnp.add, e[lo:hi]) if hi > lo else None

            def decayed(x, bb, extra):
                return x * jnp.exp(bb if extra is None else bb + extra)

            st = st_scr[h]
            qs = jnp.concatenate([decayed(q[n], b[n], span(0, n)) for n in range(ns)], axis=0)
            o_all = lax.dot_general(qs.astype(BF16), st.astype(BF16), nt_dims, preferred_element_type=F32)
            o = [o_all[r] for r in sub]
            kh = [k[j] * jnp.exp(e[j] - b[j]) for j in range(ns)]
            sc = []
            for j in range(ns - 1):
                qx = jnp.concatenate([decayed(q[n], b[n], span(j + 1, n)) for n in range(j + 1, ns)], axis=0)
                sc.append(lax.dot_general(qx.astype(BF16), kh[j].astype(BF16), nt_dims,
                                          preferred_element_type=F32))
            yield
            for j in range(ns - 1):
                ov = jnp.dot(sc[j].astype(BF16), v[j].astype(BF16), preferred_element_type=F32)
                for m, n in enumerate(range(j + 1, ns)):
                    o[n] = o[n] + ov[m * c:(m + 1) * c]
            yield
            o_h = [[x[:half], x[half:]] for x in o]
            mxu_terms = []
            for j in range(ns):
                for s in range(c):
                    ks = k_ref[i, j * c + s:j * c + s + 1, sl]
                    vs = v_ref[i, j * c + s:j * c + s + 1, sl]
                    bs = b_scr[j * c + s:j * c + s + 1, sl]
                    for p in range(2):
                        lo = p * half
                        if s >= lo + half:
                            continue
                        w = q[j][lo:lo + half] * jnp.exp(b[j][lo:lo + half] - bs) * ks
                        if s > lo:
                            w = jnp.where(row >= s - lo, w, 0.0)
                        if (s + p) % 2:
                            mxu_terms.append((j, p, vs, w))
                        else:
                            o_h[j][p] = o_h[j][p] + jnp.sum(w, axis=-1, keepdims=True) * vs
            sums = jnp.dot(jnp.concatenate([t[3] for t in mxu_terms], axis=0).astype(BF16), ones_kk,
                           preferred_element_type=F32)
            for n, (j, p, vs, _) in enumerate(mxu_terms):
                o_h[j][p] = o_h[j][p] + sums[n * half:(n + 1) * half] * vs
            o = jnp.concatenate([x for pair in o_h for x in pair], axis=0)
            yield
            kd = jnp.concatenate([kh[j] if j == ns - 1 else kh[j] * jnp.exp(span(j + 1, ns))
                                  for j in range(ns)], axis=0)
            vv = jnp.concatenate(v, axis=0)
            upd = lax.dot_general(vv.astype(BF16), kd.astype(BF16), (((0,), (0,)), ((), ())),
                                  preferred_element_type=F32)
            st_scr[h] = st * jnp.exp(span(0, ns)) + upd
            o = o * lax.rsqrt(jnp.mean(o * o, axis=-1, keepdims=True) + RMS_EPS)
            outs[h] = o * gn_ref[:, sl] * sg4[:, sl]
            yield

        heads = [head(h) for h in range(H_A)]
        for _ in range(4):
            for g in heads:
                next(g)
        o_ref[i] = jnp.concatenate(outs, axis=1).astype(o_ref.dtype)
        return carry

    lax.fori_loop(0, tb // blk, block, 0)

    @pl.when(t_blk == pl.num_programs(1) - 1)
    def _():
        for h in range(H_A):
            sout_ref[0, h] = st_scr[h].T


def _hgrn(hq, hk, lf, hv, sg, gn, s0, *, bsz, t_len, tb):
    nt = t_len // tb
    c = min(HGRN_BLOCK, tb)
    assert tb % c == 0 and c % HGRN_CHUNK == 0
    n = bsz * t_len
    row = pl.BlockSpec((tb // c, c, D_A), lambda b, t: (b * nt + t, 0, 0))
    state = pl.BlockSpec((1, H_A, K_A, V_A), lambda b, t: (b, 0, 0, 0))
    ha, s_new = pl.pallas_call(
        functools.partial(_hgrn_kernel, tb=tb),
        grid=(bsz, nt),
        in_specs=[row, row, row, row, row, _resident(gn.shape), state],
        out_specs=[row, state],
        out_shape=[jax.ShapeDtypeStruct((n // c, c, D_AV), BF16),
                   jax.ShapeDtypeStruct((bsz, H_A, K_A, V_A), F32)],
        scratch_shapes=[pltpu.VMEM((H_A, V_A, K_A), F32), pltpu.VMEM((c, D_A), F32)],
        compiler_params=_params(("parallel", "arbitrary")),
        name="hgrn",
    )(*(a.reshape(n // c, c, D_A) for a in (hq, hk, lf, hv, sg)), gn, s0)
    return ha.reshape(n, D_AV), s_new


def _shift_up(x, sh):
    row = lax.broadcasted_iota(jnp.int32, x.shape, 0)
    return jnp.where(row < V7X_SUBLANES - sh, pltpu.roll(x, V7X_SUBLANES - sh, axis=0), 1.0)


def _sb_weights(z_ref, w_ref, carry, *, q_off=None, scale=None):
    tq = z_ref.shape[1]
    new_carry = []
    for c0 in range(0, tq, V7X_LANES):
        cols = slice(c0, c0 + V7X_LANES)
        if q_off is not None:
            sub = lax.broadcasted_iota(jnp.int32, (V7X_SUBLANES, V7X_LANES), 0) * SB_NV
            lane = lax.broadcasted_iota(jnp.int32, (V7X_SUBLANES, V7X_LANES), 1) + (q_off + c0)
        run = jnp.ones((V7X_SUBLANES, V7X_LANES), F32)
        diffs = [None] * SB_NV
        for v in reversed(range(SB_NV)):
            gv = 0.5 - 0.5 * jnp.tanh(z_ref[v * V7X_SUBLANES:(v + 1) * V7X_SUBLANES, cols])
            if q_off is not None:
                gv = jnp.where(sub + v < lane, gv, 1.0)
            nxt = run * gv
            diffs[v] = run - nxt
            run = nxt
        y = _shift_up(run, 1)
        y = y * _shift_up(y, 1)
        y = y * _shift_up(y, 2)
        y = y * _shift_up(y, 4)
        offs = carry[:, cols] * y
        if scale is not None:
            offs = offs * scale
        for v in range(0, SB_NV, 2):
            pair = jnp.concatenate([diffs[v] * offs, diffs[v + 1] * offs], axis=0)
            w_ref[v * V7X_SUBLANES:(v + 2) * V7X_SUBLANES, cols] = pair.astype(BF16)
        new_carry.append(jnp.broadcast_to((offs * run)[0:1, :], run.shape))
    return jnp.concatenate(new_carry, axis=1)


def _sb_kernel(qt_ref, k_ref, vt_ref, o_ref, z_scr, w_scr, acc_scr, *, tq, q_start):
    q0 = q_start + pl.program_id(2) * tq
    n_full = q0 // SB_TK
    n_pairs = (n_full + 1) // 2
    heads = range(SB_HEADS)
    qt_all = qt_ref[0]
    head_of_row = lax.broadcasted_iota(jnp.int32, qt_all.shape, 0) // D_HB
    qts = [jnp.where(head_of_row == a, qt_all, jnp.zeros_like(qt_all)) for a in heads]

    def scores(j, slot):
        s0 = pl.multiple_of(jnp.maximum(j, 0) * SB_TK, SB_TK)
        kb = k_ref[0, pl.ds(s0, SB_TK), :]
        for a in heads:
            z_scr[slot, a] = jnp.dot(kb, qts[a], preferred_element_type=F32)

    def values(j, slot):
        s0 = pl.multiple_of(jnp.maximum(j, 0) * SB_TK, SB_TK)
        parts = [jnp.dot(vt_ref[0, a * D_HB:(a + 1) * D_HB, pl.ds(s0, SB_TK)], w_scr[slot, a],
                         preferred_element_type=F32) for a in heads]
        acc_scr[...] += jnp.concatenate(parts, axis=0)

    def weights(slot, carries, **kw):
        return tuple(_sb_weights(z_scr.at[slot, a], w_scr.at[slot, a], carries[a], **kw) for a in heads)

    scores(n_full, 1)
    scores(n_full - 1, 0)
    acc_scr[...] = jnp.zeros_like(acc_scr)
    ones = jnp.ones((V7X_SUBLANES, tq), F32)
    carries = weights(1, (ones,) * SB_HEADS, q_off=q0 - n_full * SB_TK)

    def pair(state):
        p, carries, _ = state
        j = n_full - 1 - 2 * p
        values(j + 1, 1)
        scores(j - 1, 1)
        carries = weights(0, carries)
        alive = (jnp.max(functools.reduce(jnp.maximum, carries)) > 0.0).astype(jnp.int32)
        scores(j - 2, 0)
        values(j, 0)
        valid = (j >= 1).astype(F32)
        return p + 1, weights(1, carries, scale=valid), alive

    def more(state):
        return jnp.logical_and(state[0] < n_pairs, state[2] > 0)

    n_done, _, _ = lax.while_loop(more, pair, (jnp.int32(0), carries, jnp.int32(1)))
    values(n_full - 2 * n_done, 1)
    o_ref[...] = acc_scr[...].T.astype(o_ref.dtype)


def _sb(qt, k_perm, vt_perm, *, tq, q_start):
    bsz, _, t_q = qt.shape
    s_len = k_perm.shape[1]
    nq = t_q // tq
    assert tq <= SB_TK and q_start % SB_TK == 0 and SB_TK % tq == 0
    assert s_len % SB_TK == 0 and s_len >= q_start + nq * tq - tq + SB_TK
    group = SB_HEADS * D_HB
    return pl.pallas_call(
        functools.partial(_sb_kernel, tq=tq, q_start=q_start),
        grid=(bsz, H_B // SB_HEADS, nq),
        in_specs=[pl.BlockSpec((1, group, tq), lambda b, h, i: (b, h, i)),
                  pl.BlockSpec((1, s_len, group), lambda b, h, i: (b, 0, h)),
                  pl.BlockSpec((1, group, s_len), lambda b, h, i: (b, h, 0))],
        out_specs=pl.BlockSpec((tq, group), lambda b, h, i: (b * nq + i, h)),
        out_shape=jax.ShapeDtypeStruct((bsz * t_q, D_B), BF16),
        scratch_shapes=[pltpu.VMEM((2, SB_HEADS, SB_TK, tq), F32), pltpu.VMEM((2, SB_HEADS, SB_TK, tq), BF16),
                        pltpu.VMEM((group, tq), F32)],
        compiler_params=_params(("parallel", "parallel", "arbitrary")),
        name="sb",
    )(qt, k_perm, vt_perm)


def _sb_prep_kernel(ck_ref, cv_ref, nk_ref, nv_ref, kp_ref, vt_ref, *, n_past):
    j = pl.program_id(1)
    perm = _key_permutation(False)
    perm_t = _key_permutation(True)

    def emit(n_rows, load_k, load_v):
        for blk in range(n_rows // SB_TK):
            rows = slice(blk * SB_TK, (blk + 1) * SB_TK)
            kp_ref[0, rows, :] = jnp.dot(perm, load_k(rows).astype(BF16), preferred_element_type=F32).astype(BF16)
            vt_ref[0, :, rows] = jnp.dot(load_v(rows).T.astype(BF16), perm_t,
                                         preferred_element_type=F32).astype(BF16)

    def cached(ref):
        return lambda rows: jnp.concatenate([ref[0, rows, h, :] for h in range(H_B)], axis=1)

    @pl.when(j < n_past)
    def _():
        emit(SB_PREP_ROWS, cached(ck_ref), cached(cv_ref))

    @pl.when(j == n_past)
    def _():
        emit(SB_TK, lambda rows: nk_ref[0, rows, :], lambda rows: nv_ref[0, rows, :])
        kp_ref[0, SB_TK:, :] = jnp.zeros((SB_PREP_ROWS - SB_TK, D_B), BF16)
        vt_ref[0, :, SB_TK:] = jnp.zeros((D_B, SB_PREP_ROWS - SB_TK), BF16)


def _sb_prep(cache_k, cache_v, new_k, new_v):
    bsz, p_len = cache_k.shape[:2]
    assert p_len % SB_PREP_ROWS == 0
    n_past = p_len // SB_PREP_ROWS
    past = pl.BlockSpec((1, SB_PREP_ROWS, H_B, D_HB), lambda b, j: (b, jnp.minimum(j, n_past - 1), 0, 0))
    new = pl.BlockSpec((1, SB_TK, D_B), lambda b, j: (b, 0, 0))
    return pl.pallas_call(
        functools.partial(_sb_prep_kernel, n_past=n_past),
        grid=(bsz, n_past + 1),
        in_specs=[past, past, new, new],
        out_specs=[pl.BlockSpec((1, SB_PREP_ROWS, D_B), lambda b, j: (b, j, 0)),
                   pl.BlockSpec((1, D_B, SB_PREP_ROWS), lambda b, j: (b, 0, j))],
        out_shape=[jax.ShapeDtypeStruct((bsz, p_len + SB_PREP_ROWS, D_B), BF16),
                   jax.ShapeDtypeStruct((bsz, D_B, p_len + SB_PREP_ROWS), BF16)],
        compiler_params=_params(("parallel", "arbitrary")),
        name="sb_prep",
    )(cache_k, cache_v, new_k, new_v)


def _mix_ffn_kernel(x_ref, ha_ref, hb_ref, ga_ref, gb_ref, wa_ref, wb_ref, wo_ref, g2_ref, b2_ref,
                    wg_ref, wu_ref, wd_ref, g3_ref, b3_ref, o_ref, *, alpha):
    pa = jnp.dot(ha_ref[...], wa_ref[...], preferred_element_type=F32)
    pb = jnp.dot(hb_ref[...], wb_ref[...], preferred_element_type=F32)
    merged = ga_ref[...] * pa + gb_ref[...] * pb
    mix = jnp.dot(merged.astype(BF16), wo_ref[...], preferred_element_type=F32)
    x2 = _layer_norm(alpha * x_ref[...] + mix, g2_ref[...], b2_ref[...])
    ff = _swiglu(x2.astype(BF16), wg_ref, wu_ref, wd_ref)
    o_ref[...] = _layer_norm(alpha * x2 + 0.5 * ff, g3_ref[...], b3_ref[...])


def _mix_ffn(x1, ha, hb, ga, gb, wa, wb, wo, g2, b2, wg, wu, wd, g3, b3, *, alpha, tm):
    n = x1.shape[0]

    def row(width):
        return pl.BlockSpec((tm, width), lambda i: (i, 0))

    consts = (wa, wb, wo, g2, b2, wg, wu, wd, g3, b3)
    return pl.pallas_call(
        functools.partial(_mix_ffn_kernel, alpha=alpha),
        grid=(n // tm,),
        in_specs=[row(D_MODEL), row(D_AV), row(D_B), row(D_MODEL), row(D_MODEL)]
                 + [_resident(c.shape) for c in consts],
        out_specs=row(D_MODEL),
        out_shape=jax.ShapeDtypeStruct((n, D_MODEL), F32),
        compiler_params=_params(("parallel",)),
        name="mix_ffn",
    )(x1, ha, hb, ga, gb, *consts)


def _pick_tile(n, cap):
    t = min(n, cap)
    assert n % t == 0
    return t


def _trunk(x, s0, past_k, past_v, w):
    bsz, t_len, _ = x.shape
    n = bsz * t_len
    depth = w["w_in"].shape[0]
    alpha = (2 * depth) ** 0.25
    tm = _pick_tile(n, 512)
    xf = x.reshape(n, D_MODEL)
    ks, vs, ss = [], [], []
    for l in range(depth):
        x1 = _ffn_ln(xf, w["ffn1_wg"][l], w["ffn1_wu"][l], w["ffn1_wd"][l], w["ln1_g"][l], w["ln1_b"][l],
                     alpha=alpha, tm=tm)
        sb_layout = past_k is None
        hq, hk, lf, hv, sg, kb, vb, ga, gb, *sb_ops = _inproj(
            x1, w["w_in"][l], w["lb_logits"], layer=l, tm=tm, t_len=t_len, sb_layout=sb_layout)
        ha, s_new = _hgrn(hq, hk, lf, hv, sg, w["hgrn_norm_g"][l], s0[l],
                          bsz=bsz, t_len=t_len, tb=_pick_tile(t_len, 512))

        if sb_layout:
            qt, k_perm, vt_perm = sb_ops
            hb = _sb(qt, k_perm.reshape(bsz, t_len, D_B), vt_perm, tq=SB_TK, q_start=0)
        else:
            q_start = past_k.shape[2]
            tq = V7X_LANES
            assert t_len <= tq
            pad_new = ((0, 0), (0, SB_TK - t_len), (0, 0))
            k_perm, vt_perm = _sb_prep(past_k[l], past_v[l],
                                       jnp.pad(kb.reshape(bsz, t_len, D_B), pad_new),
                                       jnp.pad(vb.reshape(bsz, t_len, D_B), pad_new))
            q3 = jnp.pad(sb_ops[0].reshape(bsz, t_len, D_B), ((0, 0), (0, tq - t_len), (0, 0)))
            hb = _sb(jnp.swapaxes(q3, 1, 2), k_perm, vt_perm, tq=tq, q_start=q_start)
            hb = hb.reshape(bsz, tq, D_B)[:, :t_len].reshape(n, D_B)

        xf = _mix_ffn(x1, ha, hb, ga, gb, w["w_branch_a"][l], w["w_branch_b"][l], w["w_out"][l],
                      w["ln2_g"][l], w["ln2_b"][l], w["ffn2_wg"][l], w["ffn2_wu"][l], w["ffn2_wd"][l],
                      w["ln3_g"][l], w["ln3_b"][l], alpha=alpha, tm=tm)
        ks.append(kb.reshape(bsz, t_len, H_B, D_HB))
        vs.append(vb.reshape(bsz, t_len, H_B, D_HB))
        ss.append(s_new)
    return xf.reshape(bsz, t_len, D_MODEL), jnp.stack(ks), jnp.stack(vs), jnp.stack(ss)


def kernel(x_prompt, x_sample, cache_sb_k, cache_sb_v, state_hgrn, ffn1_wg, ffn1_wu, ffn1_wd, ln1_g, ln1_b,
           w_in, lb_logits, hgrn_norm_g, w_branch_a, w_branch_b, w_out, ln2_g, ln2_b,
           ffn2_wg, ffn2_wu, ffn2_wd, ln3_g, ln3_b):
    depth = w_in.shape[0]

    def vec(p):
        return p[:, None, :]

    w = dict(
        ffn1_wg=ffn1_wg.astype(BF16), ffn1_wu=ffn1_wu.astype(BF16), ffn1_wd=ffn1_wd.astype(BF16),
        ln1_g=vec(ln1_g), ln1_b=vec(ln1_b), w_in=w_in.astype(BF16), lb_logits=lb_logits,
        hgrn_norm_g=vec(hgrn_norm_g), w_branch_a=w_branch_a.astype(BF16), w_branch_b=w_branch_b.astype(BF16),
        w_out=w_out.astype(BF16), ln2_g=vec(ln2_g), ln2_b=vec(ln2_b),
        ffn2_wg=ffn2_wg.astype(BF16), ffn2_wu=ffn2_wu.astype(BF16), ffn2_wd=ffn2_wd.astype(BF16),
        ln3_g=vec(ln3_g), ln3_b=vec(ln3_b))
    s0_prompt = jnp.zeros((depth, x_prompt.shape[0], H_A, K_A, V_A), F32)
    y_p, k_p, v_p, s_p = _trunk(x_prompt, s0_prompt, None, None, w)
    y_s, k_s, v_s, s_s = _trunk(x_sample, state_hgrn, cache_sb_k, cache_sb_v, w)
    return (y_p, y_s, k_p, v_p, s_p, k_s, v_s, s_s)
```

```python
import functools

import jax
import jax.numpy as jnp
from jax import lax
from jax.experimental import pallas as pl
from jax.experimental.pallas import tpu as pltpu

F32 = jnp.float32
BF16 = jnp.bfloat16

D_MODEL = 1024
H_A, K_A, V_A = 4, 128, 128
D_A = H_A * K_A
D_AV = H_A * V_A
H_B, D_HB = 8, 64
D_B = H_B * D_HB
D_FF = 2816
N_IN = 2 * D_A + 2 * D_AV + 3 * D_B + 2 * D_MODEL
LN_EPS = 1e-5
RMS_EPS = 1e-6
SB_SCALE = D_HB ** -0.5

V7X_LANES = 128
V7X_SUBLANES = 8
V7X_MXU_DIM = 256
V7X_VMEM_LIMIT_BYTES = 56 * 1024 * 1024

FF_CHUNK = V7X_MXU_DIM
HGRN_CHUNK = 16
HGRN_BLOCK = 64
SB_TK = 256
SB_NV = SB_TK // V7X_SUBLANES
SB_HEADS = 4
SB_PREP_ROWS = 4 * SB_TK


def _params(sem):
    return pltpu.CompilerParams(dimension_semantics=sem, vmem_limit_bytes=V7X_VMEM_LIMIT_BYTES)


def _resident(shape):
    nd = len(shape)
    return pl.BlockSpec(shape, lambda *_: (0,) * nd, pipeline_mode=pl.Buffered(1))


def _layer_norm(y, g, b):
    mu = jnp.mean(y, axis=-1, keepdims=True)
    d = y - mu
    var = jnp.mean(d * d, axis=-1, keepdims=True)
    return d * lax.rsqrt(var + LN_EPS) * g + b


def _silu(x):
    return x * jax.nn.sigmoid(x)


def _swiglu(xb, wg_ref, wu_ref, wd_ref):
    acc = jnp.zeros((xb.shape[0], D_MODEL), F32)
    for c in range(D_FF // FF_CHUNK):
        sl = slice(c * FF_CHUNK, (c + 1) * FF_CHUNK)
        g = jnp.dot(xb, wg_ref[:, sl], preferred_element_type=F32)
        u = jnp.dot(xb, wu_ref[:, sl], preferred_element_type=F32)
        h = (_silu(g) * u).astype(BF16)
        acc = acc + jnp.dot(h, wd_ref[sl, :], preferred_element_type=F32)
    return acc


def _ffn_ln_kernel(x_ref, wg_ref, wu_ref, wd_ref, g_ref, b_ref, o_ref, *, alpha):
    x = x_ref[...]
    ff = _swiglu(x.astype(BF16), wg_ref, wu_ref, wd_ref)
    o_ref[...] = _layer_norm(alpha * x + 0.5 * ff, g_ref[...], b_ref[...])


def _ffn_ln(x, wg, wu, wd, g, b, *, alpha, tm):
    n = x.shape[0]
    row = pl.BlockSpec((tm, D_MODEL), lambda i: (i, 0))
    return pl.pallas_call(
        functools.partial(_ffn_ln_kernel, alpha=alpha),
        grid=(n // tm,),
        in_specs=[row, _resident(wg.shape), _resident(wu.shape), _resident(wd.shape),
                  _resident(g.shape), _resident(b.shape)],
        out_specs=row,
        out_shape=jax.ShapeDtypeStruct((n, D_MODEL), F32),
        compiler_params=_params(("parallel",)),
        name="ffn_ln",
    )(x, wg, wu, wd, g, b)


_CUTS = (0, D_A, 2 * D_A, 2 * D_A + D_AV, 2 * D_A + 2 * D_AV,
         2 * D_A + 2 * D_AV + D_B, 2 * D_A + 2 * D_AV + 2 * D_B,
         2 * D_A + 2 * D_AV + 3 * D_B, 2 * D_A + 2 * D_AV + 3 * D_B + D_MODEL, N_IN)


def _key_permutation(transposed):
    i = lax.broadcasted_iota(jnp.int32, (SB_TK, SB_TK), 1 if transposed else 0)
    j = lax.broadcasted_iota(jnp.int32, (SB_TK, SB_TK), 0 if transposed else 1)
    src = (i % V7X_SUBLANES) * SB_NV + i // V7X_SUBLANES
    return jnp.where(j == src, 1.0, 0.0).astype(BF16)


def _inproj_kernel(x_ref, w_ref, lbl_ref, hq_ref, hk_ref, lf_ref, hv_ref, sg_ref,
                   kb_ref, vb_ref, ga_ref, gb_ref, *sb_refs, layer, sb_layout):
    xb = x_ref[...].astype(BF16)

    def proj(i):
        return jnp.dot(xb, w_ref[:, _CUTS[i]:_CUTS[i + 1]], preferred_element_type=F32)

    hq_ref[...] = _silu(proj(0))
    rows = [lbl_ref[i:i + 1, :] for i in range(lbl_ref.shape[0])]
    m = functools.reduce(jnp.maximum, rows)
    ex = [jnp.exp(r - m) for r in rows]
    one_minus_lb = sum(ex[layer + 1:]) / sum(ex)
    hk = one_minus_lb * jax.nn.sigmoid(-proj(1))
    hk_ref[...] = hk
    lf_ref[...] = jnp.log1p(-hk)
    hv_ref[...] = proj(2)
    sg_ref[...] = _silu(proj(3))
    ga_ref[...] = jax.nn.sigmoid(proj(7))
    gb_ref[...] = jax.nn.sigmoid(proj(8))
    qh = proj(4) * (0.5 * SB_SCALE)
    kf = proj(5)
    vf = proj(6)
    if not sb_layout:
        kb_ref[...] = kf
        vb_ref[...] = vf
        sb_refs[0][...] = qh.astype(BF16)
        return
    kb_ref[0] = kf.T
    vb_ref[0] = vf.T
    qt_ref, kp_ref, vt_ref = sb_refs
    qt_ref[0] = qh.T.astype(BF16)
    perm = _key_permutation(False)
    perm_t = _key_permutation(True)
    for blk in range(x_ref.shape[0] // SB_TK):
        rows = slice(blk * SB_TK, (blk + 1) * SB_TK)
        kp_ref[rows, :] = jnp.dot(perm, kf[rows].astype(BF16), preferred_element_type=F32).astype(BF16)
        vt_ref[0, :, rows] = jnp.dot(vf[rows].T.astype(BF16), perm_t, preferred_element_type=F32).astype(BF16)


def _inproj(x1, w_in, lb_logits, *, layer, tm, t_len, sb_layout):
    n = x1.shape[0]
    nt = t_len // tm

    def row(width):
        return pl.BlockSpec((tm, width), lambda i: (i, 0))

    cols = pl.BlockSpec((1, D_B, tm), lambda i: (i // nt, 0, i % nt))
    widths = (D_A, D_A, D_A, D_AV, D_AV, D_B, D_B, D_MODEL, D_MODEL)
    specs = [row(w) for w in widths]
    shapes = [jax.ShapeDtypeStruct((n, w), F32) for w in widths]
    if sb_layout:
        assert tm % SB_TK == 0 and t_len % tm == 0
        specs[5:7] = [cols, cols]
        shapes[5:7] = [jax.ShapeDtypeStruct((n // t_len, D_B, t_len), F32)] * 2
        specs += [cols, row(D_B), cols]
        shapes += [jax.ShapeDtypeStruct((n // t_len, D_B, t_len), BF16), jax.ShapeDtypeStruct((n, D_B), BF16),
                   jax.ShapeDtypeStruct((n // t_len, D_B, t_len), BF16)]
    else:
        specs += [row(D_B)]
        shapes += [jax.ShapeDtypeStruct((n, D_B), BF16)]
    return pl.pallas_call(
        functools.partial(_inproj_kernel, layer=layer, sb_layout=sb_layout),
        grid=(n // tm,),
        in_specs=[row(D_MODEL), _resident(w_in.shape), _resident(lb_logits.shape)],
        out_specs=specs,
        out_shape=shapes,
        compiler_params=_params(("parallel",)),
        name="inproj",
    )(x1, w_in, lb_logits)


def _cumsum_rows(x):
    n = x.shape[0]
    row = lax.broadcasted_iota(jnp.int32, x.shape, 0)
    sh = 1
    while sh < n:
        x = x + jnp.where(row >= sh, pltpu.roll(x, sh, axis=0), 0.0)
        sh *= 2
    return x


def _hgrn_kernel(q_ref, k_ref, lf_ref, v_ref, sg_ref, gn_ref, s0_ref, o_ref, sout_ref, st_scr, b_scr, *, tb):
    c = HGRN_CHUNK
    blk = q_ref.shape[1]
    ns = blk // c
    t_blk = pl.program_id(1)

    @pl.when(t_blk == 0)
    def _():
        for h in range(H_A):
            st_scr[h] = s0_ref[0, h].T

    half = V7X_SUBLANES
    row = lax.broadcasted_iota(jnp.int32, (half, K_A), 0)
    ones_kk = jnp.ones((K_A, K_A), BF16)
    nt_dims = (((1,), (1,)), ((), ()))

    def block(i, carry):
        q4 = q_ref[i]
        k4 = k_ref[i]
        v4 = v_ref[i]
        sg4 = sg_ref[i]
        lf4 = lf_ref[i]
        b4 = jnp.concatenate([_cumsum_rows(lf4[j * c:(j + 1) * c]) for j in range(ns)], axis=0)
        b_scr[...] = b4
        outs = [None] * H_A

        def head(h):
            sl = slice(h * K_A, (h + 1) * K_A)
            sub = [slice(j * c, (j + 1) * c) for j in range(ns)]
            q = [q4[r, sl] for r in sub]
            k = [k4[r, sl] for r in sub]
            v = [v4[r, sl] for r in sub]
            b = [b4[r, sl] for r in sub]
            e = [x[c - 1:c, :] for x in b]

            def span(lo, hi):
                return functools.reduce(jnp.add, e[lo:hi]) if hi > lo else None

            def decayed(x, bb, extra):
                return x * jnp.exp(bb if extra is None else bb + extra)

            st = st_scr[h]
            qs = jnp.concatenate([decayed(q[n], b[n], span(0, n)) for n in range(ns)], axis=0)
            o_all = lax.dot_general(qs.astype(BF16), st.astype(BF16), nt_dims, preferred_element_type=F32)
            o = [o_all[r] for r in sub]
            kh = [k[j] * jnp.exp(e[j] - b[j]) for j in range(ns)]
            sc = []
            for j in range(ns - 1):
                qx = jnp.concatenate([decayed(q[n], b[n], span(j + 1, n)) for n in range(j + 1, ns)], axis=0)
                sc.append(lax.dot_general(qx.astype(BF16), kh[j].astype(BF16), nt_dims,
                                          preferred_element_type=F32))
            yield
            for j in range(ns - 1):
                ov = jnp.dot(sc[j].astype(BF16), v[j].astype(BF16), preferred_element_type=F32)
                for m, n in enumerate(range(j + 1, ns)):
                    o[n] = o[n] + ov[m * c:(m + 1) * c]
            yield
            o_h = [[x[:half], x[half:]] for x in o]
            mxu_terms = []
            for j in range(ns):
                for s in range(c):
                    ks = k_ref[i, j * c + s:j * c + s + 1, sl]
                    vs = v_ref[i, j * c + s:j * c + s + 1, sl]
                    bs = b_scr[j * c + s:j * c + s + 1, sl]
                    for p in range(2):
                        lo = p * half
                        if s >= lo + half:
                            continue
                        w = q[j][lo:lo + half] * jnp.exp(b[j][lo:lo + half] - bs) * ks
                        if s > lo:
                            w = jnp.where(row >= s - lo, w, 0.0)
                        if (s + p) % 2:
                            mxu_terms.append((j, p, vs, w))
                        else:
                            o_h[j][p] = o_h[j][p] + jnp.sum(w, axis=-1, keepdims=True) * vs
            sums = jnp.dot(jnp.concatenate([t[3] for t in mxu_terms], axis=0).astype(BF16), ones_kk,
                           preferred_element_type=F32)
            for n, (j, p, vs, _) in enumerate(mxu_terms):
                o_h[j][p] = o_h[j][p] + sums[n * half:(n + 1) * half] * vs
            o = jnp.concatenate([x for pair in o_h for x in pair], axis=0)
            yield
            kd = jnp.concatenate([kh[j] if j == ns - 1 else kh[j] * jnp.exp(span(j + 1, ns))
                                  for j in range(ns)], axis=0)
            vv = jnp.concatenate(v, axis=0)
            upd = lax.dot_general(vv.astype(BF16), kd.astype(BF16), (((0,), (0,)), ((), ())),
                                  preferred_element_type=F32)
            st_scr[h] = st * jnp.exp(span(0, ns)) + upd
            o = o * lax.rsqrt(jnp.mean(o * o, axis=-1, keepdims=True) + RMS_EPS)
            outs[h] = o * gn_ref[:, sl] * sg4[:, sl]
            yield

        heads = [head(h) for h in range(H_A)]
        for _ in range(4):
            for g in heads:
                next(g)
        o_ref[i] = jnp.concatenate(outs, axis=1).astype(o_ref.dtype)
        return carry

    lax.fori_loop(0, tb // blk, block, 0)

    @pl.when(t_blk == pl.num_programs(1) - 1)
    def _():
        for h in range(H_A):
            sout_ref[0, h] = st_scr[h].T


def _hgrn(hq, hk, lf, hv, sg, gn, s0, *, bsz, t_len, tb):
    nt = t_len // tb
    c = min(HGRN_BLOCK, tb)
    assert tb % c == 0 and c % HGRN_CHUNK == 0
    n = bsz * t_len
    row = pl.BlockSpec((tb // c, c, D_A), lambda b, t: (b * nt + t, 0, 0))
    state = pl.BlockSpec((1, H_A, K_A, V_A), lambda b, t: (b, 0, 0, 0))
    ha, s_new = pl.pallas_call(
        functools.partial(_hgrn_kernel, tb=tb),
        grid=(bsz, nt),
        in_specs=[row, row, row, row, row, _resident(gn.shape), state],
        out_specs=[row, state],
        out_shape=[jax.ShapeDtypeStruct((n // c, c, D_AV), BF16),
                   jax.ShapeDtypeStruct((bsz, H_A, K_A, V_A), F32)],
        scratch_shapes=[pltpu.VMEM((H_A, V_A, K_A), F32), pltpu.VMEM((c, D_A), F32)],
        compiler_params=_params(("parallel", "arbitrary")),
        name="hgrn",
    )(*(a.reshape(n // c, c, D_A) for a in (hq, hk, lf, hv, sg)), gn, s0)
    return ha.reshape(n, D_AV), s_new


def _shift_up(x, sh):
    row = lax.broadcasted_iota(jnp.int32, x.shape, 0)
    return jnp.where(row < V7X_SUBLANES - sh, pltpu.roll(x, V7X_SUBLANES - sh, axis=0), 1.0)


def _sb_weights(z_ref, w_ref, carry, *, q_off=None, scale=None):
    tq = z_ref.shape[1]
    new_carry = []
    for c0 in range(0, tq, V7X_LANES):
        cols = slice(c0, c0 + V7X_LANES)
        if q_off is not None:
            sub = lax.broadcasted_iota(jnp.int32, (V7X_SUBLANES, V7X_LANES), 0) * SB_NV
            lane = lax.broadcasted_iota(jnp.int32, (V7X_SUBLANES, V7X_LANES), 1) + (q_off + c0)
        run = jnp.ones((V7X_SUBLANES, V7X_LANES), F32)
        diffs = [None] * SB_NV
        for v in reversed(range(SB_NV)):
            gv = 0.5 - 0.5 * jnp.tanh(z_ref[v * V7X_SUBLANES:(v + 1) * V7X_SUBLANES, cols])
            if q_off is not None:
                gv = jnp.where(sub + v < lane, gv, 1.0)
            nxt = run * gv
            diffs[v] = run - nxt
            run = nxt
        y = _shift_up(run, 1)
        y = y * _shift_up(y, 1)
        y = y * _shift_up(y, 2)
        y = y * _shift_up(y, 4)
        offs = carry[:, cols] * y
        if scale is not None:
            offs = offs * scale
        for v in range(0, SB_NV, 2):
            pair = jnp.concatenate([diffs[v] * offs, diffs[v + 1] * offs], axis=0)
            w_ref[v * V7X_SUBLANES:(v + 2) * V7X_SUBLANES, cols] = pair.astype(BF16)
        new_carry.append(jnp.broadcast_to((offs * run)[0:1, :], run.shape))
    return jnp.concatenate(new_carry, axis=1)


def _sb_kernel(qt_ref, k_ref, vt_ref, o_ref, z_scr, w_scr, acc_scr, *, tq, q_start):
    q0 = q_start + pl.program_id(2) * tq
    n_full = q0 // SB_TK
    n_pairs = (n_full + 1) // 2
    heads = range(SB_HEADS)
    qt_all = qt_ref[0]
    head_of_row = lax.broadcasted_iota(jnp.int32, qt_all.shape, 0) // D_HB
    qts = [jnp.where(head_of_row == a, qt_all, jnp.zeros_like(qt_all)) for a in heads]

    def scores(j, slot):
        s0 = pl.multiple_of(jnp.maximum(j, 0) * SB_TK, SB_TK)
        kb = k_ref[0, pl.ds(s0, SB_TK), :]
        for a in heads:
            z_scr[slot, a] = jnp.dot(kb, qts[a], preferred_element_type=F32)

    def values(j, slot):
        s0 = pl.multiple_of(jnp.maximum(j, 0) * SB_TK, SB_TK)
        parts = [jnp.dot(vt_ref[0, a * D_HB:(a + 1) * D_HB, pl.ds(s0, SB_TK)], w_scr[slot, a],
                         preferred_element_type=F32) for a in heads]
        acc_scr[...] += jnp.concatenate(parts, axis=0)

    def weights(slot, carries, **kw):
        return tuple(_sb_weights(z_scr.at[slot, a], w_scr.at[slot, a], carries[a], **kw) for a in heads)

    scores(n_full, 1)
    scores(n_full - 1, 0)
    acc_scr[...] = jnp.zeros_like(acc_scr)
    ones = jnp.ones((V7X_SUBLANES, tq), F32)
    carries = weights(1, (ones,) * SB_HEADS, q_off=q0 - n_full * SB_TK)

    def pair(state):
        p, carries, _ = state
        j = n_full - 1 - 2 * p
        values(j + 1, 1)
        scores(j - 1, 1)
        carries = weights(0, carries)
        alive = (jnp.max(functools.reduce(jnp.maximum, carries)) > 0.0).astype(jnp.int32)
        scores(j - 2, 0)
        values(j, 0)
        valid = (j >= 1).astype(F32)
        return p + 1, weights(1, carries, scale=valid), alive

    def more(state):
        return jnp.logical_and(state[0] < n_pairs, state[2] > 0)

    n_done, _, _ = lax.while_loop(more, pair, (jnp.int32(0), carries, jnp.int32(1)))
    values(n_full - 2 * n_done, 1)
    o_ref[...] = acc_scr[...].T.astype(o_ref.dtype)


def _sb(qt, k_perm, vt_perm, *, tq, q_start):
    bsz, _, t_q = qt.shape
    s_len = k_perm.shape[1]
    nq = t_q // tq
    assert tq <= SB_TK and q_start % SB_TK == 0 and SB_TK % tq == 0
    assert s_len % SB_TK == 0 and s_len >= q_start + nq * tq - tq + SB_TK
    group = SB_HEADS * D_HB
    return pl.pallas_call(
        functools.partial(_sb_kernel, tq=tq, q_start=q_start),
        grid=(bsz, H_B // SB_HEADS, nq),
        in_specs=[pl.BlockSpec((1, group, tq), lambda b, h, i: (b, h, i)),
                  pl.BlockSpec((1, s_len, group), lambda b, h, i: (b, 0, h)),
                  pl.BlockSpec((1, group, s_len), lambda b, h, i: (b, h, 0))],
        out_specs=pl.BlockSpec((tq, group), lambda b, h, i: (b * nq + i, h)),
        out_shape=jax.ShapeDtypeStruct((bsz * t_q, D_B), BF16),
        scratch_shapes=[pltpu.VMEM((2, SB_HEADS, SB_TK, tq), F32), pltpu.VMEM((2, SB_HEADS, SB_TK, tq), BF16),
                        pltpu.VMEM((group, tq), F32)],
        compiler_params=_params(("parallel", "parallel", "arbitrary")),
        name="sb",
    )(qt, k_perm, vt_perm)


def _sb_prep_kernel(ck_ref, cv_ref, nk_ref, nv_ref, kp_ref, vt_ref, *, n_past):
    j = pl.program_id(1)
    perm = _key_permutation(False)
    perm_t = _key_permutation(True)

    @pl.when(j < n_past)
    def _():
        for blk in range(SB_PREP_ROWS // SB_TK):
            cols = slice(blk * SB_TK, (blk + 1) * SB_TK)
            kp_ref[0, cols, :] = lax.dot_general(perm, ck_ref[0, :, cols].astype(BF16), (((1,), (1,)), ((), ())),
                                                 preferred_element_type=F32).astype(BF16)
            vt_ref[0, :, cols] = jnp.dot(cv_ref[0, :, cols].astype(BF16), perm_t,
                                         preferred_element_type=F32).astype(BF16)

    @pl.when(j == n_past)
    def _():
        kp_ref[0, :SB_TK, :] = jnp.dot(perm, nk_ref[0].astype(BF16), preferred_element_type=F32).astype(BF16)
        vt_ref[0, :, :SB_TK] = jnp.dot(nv_ref[0].T.astype(BF16), perm_t, preferred_element_type=F32).astype(BF16)
        kp_ref[0, SB_TK:, :] = jnp.zeros((SB_PREP_ROWS - SB_TK, D_B), BF16)
        vt_ref[0, :, SB_TK:] = jnp.zeros((D_B, SB_PREP_ROWS - SB_TK), BF16)


def _sb_prep(cache_kt, cache_vt, new_k, new_v):
    bsz, _, p_len = cache_kt.shape
    assert p_len % SB_PREP_ROWS == 0
    n_past = p_len // SB_PREP_ROWS
    past = pl.BlockSpec((1, D_B, SB_PREP_ROWS), lambda b, j: (b, 0, jnp.minimum(j, n_past - 1)))
    new = pl.BlockSpec((1, SB_TK, D_B), lambda b, j: (b, 0, 0))
    return pl.pallas_call(
        functools.partial(_sb_prep_kernel, n_past=n_past),
        grid=(bsz, n_past + 1),
        in_specs=[past, past, new, new],
        out_specs=[pl.BlockSpec((1, SB_PREP_ROWS, D_B), lambda b, j: (b, j, 0)),
                   pl.BlockSpec((1, D_B, SB_PREP_ROWS), lambda b, j: (b, 0, j))],
        out_shape=[jax.ShapeDtypeStruct((bsz, p_len + SB_PREP_ROWS, D_B), BF16),
                   jax.ShapeDtypeStruct((bsz, D_B, p_len + SB_PREP_ROWS), BF16)],
        compiler_params=_params(("parallel", "arbitrary")),
        name="sb_prep",
    )(cache_kt, cache_vt, new_k, new_v)


def _mix_ffn_kernel(x_ref, ha_ref, hb_ref, ga_ref, gb_ref, wa_ref, wb_ref, wo_ref, g2_ref, b2_ref,
                    wg_ref, wu_ref, wd_ref, g3_ref, b3_ref, o_ref, *, alpha):
    pa = jnp.dot(ha_ref[...], wa_ref[...], preferred_element_type=F32)
    pb = jnp.dot(hb_ref[...], wb_ref[...], preferred_element_type=F32)
    merged = ga_ref[...] * pa + gb_ref[...] * pb
    mix = jnp.dot(merged.astype(BF16), wo_ref[...], preferred_element_type=F32)
    x2 = _layer_norm(alpha * x_ref[...] + mix, g2_ref[...], b2_ref[...])
    ff = _swiglu(x2.astype(BF16), wg_ref, wu_ref, wd_ref)
    o_ref[...] = _layer_norm(alpha * x2 + 0.5 * ff, g3_ref[...], b3_ref[...])


def _mix_ffn(x1, ha, hb, ga, gb, wa, wb, wo, g2, b2, wg, wu, wd, g3, b3, *, alpha, tm):
    n = x1.shape[0]

    def row(width):
        return pl.BlockSpec((tm, width), lambda i: (i, 0))

    consts = (wa, wb, wo, g2, b2, wg, wu, wd, g3, b3)
    return pl.pallas_call(
        functools.partial(_mix_ffn_kernel, alpha=alpha),
        grid=(n // tm,),
        in_specs=[row(D_MODEL), row(D_AV), row(D_B), row(D_MODEL), row(D_MODEL)]
                 + [_resident(c.shape) for c in consts],
        out_specs=row(D_MODEL),
        out_shape=jax.ShapeDtypeStruct((n, D_MODEL), F32),
        compiler_params=_params(("parallel",)),
        name="mix_ffn",
    )(x1, ha, hb, ga, gb, *consts)


def _channels_first(a):
    bsz, p_len = a.shape[:2]
    return jnp.transpose(a, (0, 2, 3, 1)).reshape(bsz, D_B, p_len)


def _pick_tile(n, cap):
    t = min(n, cap)
    assert n % t == 0
    return t


def _trunk(x, s0, past_k, past_v, w):
    bsz, t_len, _ = x.shape
    n = bsz * t_len
    depth = w["w_in"].shape[0]
    alpha = (2 * depth) ** 0.25
    tm = _pick_tile(n, 512)
    xf = x.reshape(n, D_MODEL)
    ks, vs, ss = [], [], []
    for l in range(depth):
        x1 = _ffn_ln(xf, w["ffn1_wg"][l], w["ffn1_wu"][l], w["ffn1_wd"][l], w["ln1_g"][l], w["ln1_b"][l],
                     alpha=alpha, tm=tm)
        sb_layout = past_k is None
        hq, hk, lf, hv, sg, kb, vb, ga, gb, *sb_ops = _inproj(
            x1, w["w_in"][l], w["lb_logits"], layer=l, tm=tm, t_len=t_len, sb_layout=sb_layout)
        ha, s_new = _hgrn(hq, hk, lf, hv, sg, w["hgrn_norm_g"][l], s0[l],
                          bsz=bsz, t_len=t_len, tb=_pick_tile(t_len, 512))

        if sb_layout:
            qt, k_perm, vt_perm = sb_ops
            hb = _sb(qt, k_perm.reshape(bsz, t_len, D_B), vt_perm, tq=SB_TK, q_start=0)
        else:
            q_start = past_k.shape[2]
            tq = V7X_LANES
            assert t_len <= tq
            pad_new = ((0, 0), (0, SB_TK - t_len), (0, 0))
            k_perm, vt_perm = _sb_prep(_channels_first(past_k[l]), _channels_first(past_v[l]),
                                       jnp.pad(kb.reshape(bsz, t_len, D_B), pad_new),
                                       jnp.pad(vb.reshape(bsz, t_len, D_B), pad_new))
            q3 = jnp.pad(sb_ops[0].reshape(bsz, t_len, D_B), ((0, 0), (0, tq - t_len), (0, 0)))
            hb = _sb(jnp.swapaxes(q3, 1, 2), k_perm, vt_perm, tq=tq, q_start=q_start)
            hb = hb.reshape(bsz, tq, D_B)[:, :t_len].reshape(n, D_B)

        xf = _mix_ffn(x1, ha, hb, ga, gb, w["w_branch_a"][l], w["w_branch_b"][l], w["w_out"][l],
                      w["ln2_g"][l], w["ln2_b"][l], w["ffn2_wg"][l], w["ffn2_wu"][l], w["ffn2_wd"][l],
                      w["ln3_g"][l], w["ln3_b"][l], alpha=alpha, tm=tm)
        if sb_layout:
            kb, vb = (jnp.transpose(a.reshape(bsz, H_B, D_HB, t_len), (0, 3, 1, 2)) for a in (kb, vb))
        ks.append(kb.reshape(bsz, t_len, H_B, D_HB))
        vs.append(vb.reshape(bsz, t_len, H_B, D_HB))
        ss.append(s_new)
    return xf.reshape(bsz, t_len, D_MODEL), jnp.stack(ks), jnp.stack(vs), jnp.stack(ss)


def kernel(x_prompt, x_sample, cache_sb_k, cache_sb_v, state_hgrn, ffn1_wg, ffn1_wu, ffn1_wd, ln1_g, ln1_b,
           w_in, lb_logits, hgrn_norm_g, w_branch_a, w_branch_b, w_out, ln2_g, ln2_b,
           ffn2_wg, ffn2_wu, ffn2_wd, ln3_g, ln3_b):
    depth = w_in.shape[0]

    def vec(p):
        return p[:, None, :]

    w = dict(
        ffn1_wg=ffn1_wg.astype(BF16), ffn1_wu=ffn1_wu.astype(BF16), ffn1_wd=ffn1_wd.astype(BF16),
        ln1_g=vec(ln1_g), ln1_b=vec(ln1_b), w_in=w_in.astype(BF16), lb_logits=lb_logits,
        hgrn_norm_g=vec(hgrn_norm_g), w_branch_a=w_branch_a.astype(BF16), w_branch_b=w_branch_b.astype(BF16),
        w_out=w_out.astype(BF16), ln2_g=vec(ln2_g), ln2_b=vec(ln2_b),
        ffn2_wg=ffn2_wg.astype(BF16), ffn2_wu=ffn2_wu.astype(BF16), ffn2_wd=ffn2_wd.astype(BF16),
        ln3_g=vec(ln3_g), ln3_b=vec(ln3_b))
    s0_prompt = jnp.zeros((depth, x_prompt.shape[0], H_A, K_A, V_A), F32)
    y_p, k_p, v_p, s_p = _trunk(x_prompt, s0_prompt, None, None, w)
    y_s, k_s, v_s, s_s = _trunk(x_sample, state_hgrn, cache_sb_k, cache_sb_v, w)
    return (y_p, y_s, k_p, v_p, s_p, k_s, v_s, s_s)
```

```python
import functools

import jax
import jax.numpy as jnp
from jax import lax
from jax.experimental import pallas as pl
from jax.experimental.pallas import tpu as pltpu

F32 = jnp.float32
BF16 = jnp.bfloat16

D_MODEL = 1024
H_A, K_A, V_A = 4, 128, 128
D_A = H_A * K_A
D_AV = H_A * V_A
H_B, D_HB = 8, 64
D_B = H_B * D_HB
D_FF = 2816
N_IN = 2 * D_A + 2 * D_AV + 3 * D_B + 2 * D_MODEL
LN_EPS = 1e-5
RMS_EPS = 1e-6
SB_SCALE = D_HB ** -0.5

V7X_LANES = 128
V7X_SUBLANES = 8
V7X_MXU_DIM = 256
V7X_VMEM_LIMIT_BYTES = 56 * 1024 * 1024

FF_CHUNK = V7X_MXU_DIM
HGRN_CHUNK = 16
HGRN_BLOCK = 64
SB_TK = 256
SB_NV = SB_TK // V7X_SUBLANES
SB_HEADS = 4
SB_PREP_ROWS = 4 * SB_TK


def _params(sem):
    return pltpu.CompilerParams(dimension_semantics=sem, vmem_limit_bytes=V7X_VMEM_LIMIT_BYTES)


def _resident(shape):
    nd = len(shape)
    return pl.BlockSpec(shape, lambda *_: (0,) * nd, pipeline_mode=pl.Buffered(1))


def _layer_norm(y, g, b):
    mu = jnp.mean(y, axis=-1, keepdims=True)
    d = y - mu
    var = jnp.mean(d * d, axis=-1, keepdims=True)
    return d * lax.rsqrt(var + LN_EPS) * g + b


def _silu(x):
    return x * jax.nn.sigmoid(x)


def _swiglu(xb, wg_ref, wu_ref, wd_ref):
    acc = jnp.zeros((xb.shape[0], D_MODEL), F32)
    for c in range(D_FF // FF_CHUNK):
        sl = slice(c * FF_CHUNK, (c + 1) * FF_CHUNK)
        g = jnp.dot(xb, wg_ref[:, sl], preferred_element_type=F32)
        u = jnp.dot(xb, wu_ref[:, sl], preferred_element_type=F32)
        h = (_silu(g) * u).astype(BF16)
        acc = acc + jnp.dot(h, wd_ref[sl, :], preferred_element_type=F32)
    return acc


def _ffn_ln_kernel(x_ref, wg_ref, wu_ref, wd_ref, g_ref, b_ref, o_ref, *, alpha):
    x = x_ref[...]
    ff = _swiglu(x.astype(BF16), wg_ref, wu_ref, wd_ref)
    o_ref[...] = _layer_norm(alpha * x + 0.5 * ff, g_ref[...], b_ref[...])


def _ffn_ln(x, wg, wu, wd, g, b, *, alpha, tm):
    n = x.shape[0]
    row = pl.BlockSpec((tm, D_MODEL), lambda i: (i, 0))
    return pl.pallas_call(
        functools.partial(_ffn_ln_kernel, alpha=alpha),
        grid=(n // tm,),
        in_specs=[row, _resident(wg.shape), _resident(wu.shape), _resident(wd.shape),
                  _resident(g.shape), _resident(b.shape)],
        out_specs=row,
        out_shape=jax.ShapeDtypeStruct((n, D_MODEL), F32),
        compiler_params=_params(("parallel",)),
        name="ffn_ln",
    )(x, wg, wu, wd, g, b)


_CUTS = (0, D_A, 2 * D_A, 2 * D_A + D_AV, 2 * D_A + 2 * D_AV,
         2 * D_A + 2 * D_AV + D_B, 2 * D_A + 2 * D_AV + 2 * D_B,
         2 * D_A + 2 * D_AV + 3 * D_B, 2 * D_A + 2 * D_AV + 3 * D_B + D_MODEL, N_IN)


def _key_permutation(transposed):
    i = lax.broadcasted_iota(jnp.int32, (SB_TK, SB_TK), 1 if transposed else 0)
    j = lax.broadcasted_iota(jnp.int32, (SB_TK, SB_TK), 0 if transposed else 1)
    src = (i % V7X_SUBLANES) * SB_NV + i // V7X_SUBLANES
    return jnp.where(j == src, 1.0, 0.0).astype(BF16)


def _inproj_kernel(x_ref, w_ref, lbl_ref, hq_ref, hk_ref, lf_ref, hv_ref, sg_ref,
                   kb_ref, vb_ref, ga_ref, gb_ref, *sb_refs, layer, sb_layout):
    xb = x_ref[...].astype(BF16)

    def proj(i):
        return jnp.dot(xb, w_ref[:, _CUTS[i]:_CUTS[i + 1]], preferred_element_type=F32)

    hq_ref[...] = _silu(proj(0))
    rows = [lbl_ref[i:i + 1, :] for i in range(lbl_ref.shape[0])]
    m = functools.reduce(jnp.maximum, rows)
    ex = [jnp.exp(r - m) for r in rows]
    one_minus_lb = sum(ex[layer + 1:]) / sum(ex)
    hk = one_minus_lb * jax.nn.sigmoid(-proj(1))
    hk_ref[...] = hk
    lf_ref[...] = jnp.log1p(-hk)
    hv_ref[...] = proj(2)
    sg_ref[...] = _silu(proj(3))
    ga_ref[...] = jax.nn.sigmoid(proj(7))
    gb_ref[...] = jax.nn.sigmoid(proj(8))
    qh = proj(4) * (0.5 * SB_SCALE)
    kf = proj(5)
    vf = proj(6)
    if not sb_layout:
        kb_ref[...] = kf
        vb_ref[...] = vf
        sb_refs[0][...] = qh.astype(BF16)
        return
    kb_ref[0] = kf.T
    vb_ref[0] = vf.T
    qt_ref, kp_ref, vt_ref = sb_refs
    qt_ref[0] = qh.T.astype(BF16)
    perm = _key_permutation(False)
    perm_t = _key_permutation(True)
    for blk in range(x_ref.shape[0] // SB_TK):
        rows = slice(blk * SB_TK, (blk + 1) * SB_TK)
        kp_ref[rows, :] = jnp.dot(perm, kf[rows].astype(BF16), preferred_element_type=F32).astype(BF16)
        vt_ref[0, :, rows] = jnp.dot(vf[rows].T.astype(BF16), perm_t, preferred_element_type=F32).astype(BF16)


def _inproj(x1, w_in, lb_logits, *, layer, tm, t_len, sb_layout):
    n = x1.shape[0]
    nt = t_len // tm

    def row(width):
        return pl.BlockSpec((tm, width), lambda i: (i, 0))

    cols = pl.BlockSpec((1, D_B, tm), lambda i: (i // nt, 0, i % nt))
    widths = (D_A, D_A, D_A, D_AV, D_AV, D_B, D_B, D_MODEL, D_MODEL)
    specs = [row(w) for w in widths]
    shapes = [jax.ShapeDtypeStruct((n, w), F32) for w in widths]
    if sb_layout:
        assert tm % SB_TK == 0 and t_len % tm == 0
        specs[5:7] = [cols, cols]
        shapes[5:7] = [jax.ShapeDtypeStruct((n // t_len, D_B, t_len), F32)] * 2
        specs += [cols, row(D_B), cols]
        shapes += [jax.ShapeDtypeStruct((n // t_len, D_B, t_len), BF16), jax.ShapeDtypeStruct((n, D_B), BF16),
                   jax.ShapeDtypeStruct((n // t_len, D_B, t_len), BF16)]
    else:
        specs += [row(D_B)]
        shapes += [jax.ShapeDtypeStruct((n, D_B), BF16)]
    return pl.pallas_call(
        functools.partial(_inproj_kernel, layer=layer, sb_layout=sb_layout),
        grid=(n // tm,),
        in_specs=[row(D_MODEL), _resident(w_in.shape), _resident(lb_logits.shape)],
        out_specs=specs,
        out_shape=shapes,
        compiler_params=_params(("parallel",)),
        name="inproj",
    )(x1, w_in, lb_logits)


def _cumsum_rows(x):
    n = x.shape[0]
    row = lax.broadcasted_iota(jnp.int32, x.shape, 0)
    sh = 1
    while sh < n:
        x = x + jnp.where(row >= sh, pltpu.roll(x, sh, axis=0), 0.0)
        sh *= 2
    return x


def _hgrn_kernel(q_ref, k_ref, lf_ref, v_ref, sg_ref, gn_ref, s0_ref, o_ref, sout_ref, st_scr, b_scr, *, tb):
    c = HGRN_CHUNK
    blk = q_ref.shape[1]
    ns = blk // c
    t_blk = pl.program_id(1)

    @pl.when(t_blk == 0)
    def _():
        for h in range(H_A):
            st_scr[h] = s0_ref[0, h].T

    half = V7X_SUBLANES
    row = lax.broadcasted_iota(jnp.int32, (half, K_A), 0)
    ones_kk = jnp.ones((K_A, K_A), BF16)
    nt_dims = (((1,), (1,)), ((), ()))

    def block(i, carry):
        q4 = q_ref[i]
        k4 = k_ref[i]
        v4 = v_ref[i]
        sg4 = sg_ref[i]
        lf4 = lf_ref[i]
        b4 = jnp.concatenate([_cumsum_rows(lf4[j * c:(j + 1) * c]) for j in range(ns)], axis=0)
        b_scr[...] = b4
        outs = [None] * H_A

        def head(h):
            sl = slice(h * K_A, (h + 1) * K_A)
            sub = [slice(j * c, (j + 1) * c) for j in range(ns)]
            q = [q4[r, sl] for r in sub]
            k = [k4[r, sl] for r in sub]
            v = [v4[r, sl] for r in sub]
            b = [b4[r, sl] for r in sub]
            e = [x[c - 1:c, :] for x in b]

            def span(lo, hi):
                return functools.reduce(jnp.add, e[lo:hi]) if hi > lo else None

            def decayed(x, bb, extra):
                return x * jnp.exp(bb if extra is None else bb + extra)

            st = st_scr[h]
            qs = jnp.concatenate([decayed(q[n], b[n], span(0, n)) for n in range(ns)], axis=0)
            o_all = lax.dot_general(qs.astype(BF16), st.astype(BF16), nt_dims, preferred_element_type=F32)
            o = [o_all[r] for r in sub]
            kh = [k[j] * jnp.exp(e[j] - b[j]) for j in range(ns)]
            sc = []
            for j in range(ns - 1):
                qx = jnp.concatenate([decayed(q[n], b[n], span(j + 1, n)) for n in range(j + 1, ns)], axis=0)
                sc.append(lax.dot_general(qx.astype(BF16), kh[j].astype(BF16), nt_dims,
                                          preferred_element_type=F32))
            yield
            for j in range(ns - 1):
                ov = jnp.dot(sc[j].astype(BF16), v[j].astype(BF16), preferred_element_type=F32)
                for m, n in enumerate(range(j + 1, ns)):
                    o[n] = o[n] + ov[m * c:(m + 1) * c]
            yield
            o_h = [[x[:half], x[half:]] for x in o]
            mxu_terms = []
            for j in range(ns):
                for s in range(c):
                    ks = k_ref[i, j * c + s:j * c + s + 1, sl]
                    vs = v_ref[i, j * c + s:j * c + s + 1, sl]
                    bs = b_scr[j * c + s:j * c + s + 1, sl]
                    for p in range(2):
                        lo = p * half
                        if s >= lo + half:
                            continue
                        w = q[j][lo:lo + half] * jnp.exp(b[j][lo:lo + half] - bs) * ks
                        if s > lo:
                            w = jnp.where(row >= s - lo, w, 0.0)
                        if (s + p) % 2:
                            mxu_terms.append((j, p, vs, w))
                        else:
                            o_h[j][p] = o_h[j][p] + jnp.sum(w, axis=-1, keepdims=True) * vs
            sums = jnp.dot(jnp.concatenate([t[3] for t in mxu_terms], axis=0).astype(BF16), ones_kk,
                           preferred_element_type=F32)
            for n, (j, p, vs, _) in enumerate(mxu_terms):
                o_h[j][p] = o_h[j][p] + sums[n * half:(n + 1) * half] * vs
            o = jnp.concatenate([x for pair in o_h for x in pair], axis=0)
            yield
            kd = jnp.concatenate([kh[j] if j == ns - 1 else kh[j] * jnp.exp(span(j + 1, ns))
                                  for j in range(ns)], axis=0)
            vv = jnp.concatenate(v, axis=0)
            upd = lax.dot_general(vv.astype(BF16), kd.astype(BF16), (((0,), (0,)), ((), ())),
                                  preferred_element_type=F32)
            st_scr[h] = st * jnp.exp(span(0, ns)) + upd
            o = o * lax.rsqrt(jnp.mean(o * o, axis=-1, keepdims=True) + RMS_EPS)
            outs[h] = o * gn_ref[:, sl] * sg4[:, sl]
            yield

        heads = [head(h) for h in range(H_A)]
        for _ in range(4):
            for g in heads:
                next(g)
        o_ref[i] = jnp.concatenate(outs, axis=1).astype(o_ref.dtype)
        return carry

    lax.fori_loop(0, tb // blk, block, 0)

    @pl.when(t_blk == pl.num_programs(1) - 1)
    def _():
        for h in range(H_A):
            sout_ref[0, h] = st_scr[h].T


def _hgrn(hq, hk, lf, hv, sg, gn, s0, *, bsz, t_len, tb):
    nt = t_len // tb
    c = min(HGRN_BLOCK, tb)
    assert tb % c == 0 and c % HGRN_CHUNK == 0
    n = bsz * t_len
    row = pl.BlockSpec((tb // c, c, D_A), lambda b, t: (b * nt + t, 0, 0))
    state = pl.BlockSpec((1, H_A, K_A, V_A), lambda b, t: (b, 0, 0, 0))
    ha, s_new = pl.pallas_call(
        functools.partial(_hgrn_kernel, tb=tb),
        grid=(bsz, nt),
        in_specs=[row, row, row, row, row, _resident(gn.shape), state],
        out_specs=[row, state],
        out_shape=[jax.ShapeDtypeStruct((n // c, c, D_AV), BF16),
                   jax.ShapeDtypeStruct((bsz, H_A, K_A, V_A), F32)],
        scratch_shapes=[pltpu.VMEM((H_A, V_A, K_A), F32), pltpu.VMEM((c, D_A), F32)],
        compiler_params=_params(("parallel", "arbitrary")),
        name="hgrn",
    )(*(a.reshape(n // c, c, D_A) for a in (hq, hk, lf, hv, sg)), gn, s0)
    return ha.reshape(n, D_AV), s_new


def _shift_up(x, sh):
    row = lax.broadcasted_iota(jnp.int32, x.shape, 0)
    return jnp.where(row < V7X_SUBLANES - sh, pltpu.roll(x, V7X_SUBLANES - sh, axis=0), 1.0)


def _sb_weights(z_ref, w_ref, carry, *, q_off=None, scale=None):
    tq = z_ref.shape[1]
    new_carry = []
    for c0 in range(0, tq, V7X_LANES):
        cols = slice(c0, c0 + V7X_LANES)
        if q_off is not None:
            sub = lax.broadcasted_iota(jnp.int32, (V7X_SUBLANES, V7X_LANES), 0) * SB_NV
            lane = lax.broadcasted_iota(jnp.int32, (V7X_SUBLANES, V7X_LANES), 1) + (q_off + c0)
        run = jnp.ones((V7X_SUBLANES, V7X_LANES), F32)
        diffs = [None] * SB_NV
        for v in reversed(range(SB_NV)):
            gv = 0.5 - 0.5 * jnp.tanh(z_ref[v * V7X_SUBLANES:(v + 1) * V7X_SUBLANES, cols])
            if q_off is not None:
                gv = jnp.where(sub + v < lane, gv, 1.0)
            nxt = run * gv
            diffs[v] = run - nxt
            run = nxt
        y = _shift_up(run, 1)
        y = y * _shift_up(y, 1)
        y = y * _shift_up(y, 2)
        y = y * _shift_up(y, 4)
        offs = carry[:, cols] * y
        if scale is not None:
            offs = offs * scale
        for v in range(0, SB_NV, 2):
            pair = jnp.concatenate([diffs[v] * offs, diffs[v + 1] * offs], axis=0)
            w_ref[v * V7X_SUBLANES:(v + 2) * V7X_SUBLANES, cols] = pair.astype(BF16)
        new_carry.append(jnp.broadcast_to((offs * run)[0:1, :], run.shape))
    return jnp.concatenate(new_carry, axis=1)


def _sb_kernel(qt_ref, k_ref, vt_ref, o_ref, z_scr, w_scr, acc_scr, *, tq, q_start):
    q0 = q_start + pl.program_id(2) * tq
    n_full = q0 // SB_TK
    n_pairs = jnp.maximum(n_full, 1) // 2
    heads = range(SB_HEADS)
    qt_all = qt_ref[0]
    head_of_row = lax.broadcasted_iota(jnp.int32, qt_all.shape, 0) // D_HB
    qts = [jnp.where(head_of_row == a, qt_all, jnp.zeros_like(qt_all)) for a in heads]

    def scores(j, slot):
        s0 = pl.multiple_of(jnp.maximum(j, 0) * SB_TK, SB_TK)
        kb = k_ref[0, pl.ds(s0, SB_TK), :]
        for a in heads:
            z_scr[slot, a] = jnp.dot(kb, qts[a], preferred_element_type=F32)

    def values(j, slot):
        s0 = pl.multiple_of(jnp.maximum(j, 0) * SB_TK, SB_TK)
        parts = [jnp.dot(vt_ref[0, a * D_HB:(a + 1) * D_HB, pl.ds(s0, SB_TK)], w_scr[slot, a],
                         preferred_element_type=F32) for a in heads]
        acc_scr[...] += jnp.concatenate(parts, axis=0)

    def weights(slot, carries, **kw):
        return tuple(_sb_weights(z_scr.at[slot, a], w_scr.at[slot, a], carries[a], **kw) for a in heads)

    scores(n_full, 1)
    scores(n_full - 1, 0)
    acc_scr[...] = jnp.zeros_like(acc_scr)
    ones = jnp.ones((V7X_SUBLANES, tq), F32)
    carries = weights(1, (ones,) * SB_HEADS, q_off=q0 - n_full * SB_TK)
    values(n_full, 1)
    scores(n_full - 2, 1)
    carries = weights(0, carries, scale=(n_full >= 1).astype(F32))

    def any_left(carries):
        return (jnp.max(functools.reduce(jnp.maximum, carries)) > 0.0).astype(jnp.int32)

    def pair(state):
        p, carries, _ = state
        j = n_full - 2 - 2 * p
        values(j + 1, 0)
        scores(j - 1, 0)
        carries = weights(1, carries)
        alive = any_left(carries)
        scores(j - 2, 1)
        values(j, 1)
        valid = (j >= 1).astype(F32)
        return p + 1, weights(0, carries, scale=valid), alive

    def more(state):
        return jnp.logical_and(state[0] < n_pairs, state[2] > 0)

    n_done, _, _ = lax.while_loop(more, pair, (jnp.int32(0), carries, any_left(carries)))
    values(n_full - 1 - 2 * n_done, 0)
    o_ref[...] = acc_scr[...].T.astype(o_ref.dtype)


def _sb(qt, k_perm, vt_perm, *, tq, q_start):
    bsz, _, t_q = qt.shape
    s_len = k_perm.shape[1]
    nq = t_q // tq
    assert tq <= SB_TK and q_start % SB_TK == 0 and SB_TK % tq == 0
    assert s_len % SB_TK == 0 and s_len >= q_start + nq * tq - tq + SB_TK
    group = SB_HEADS * D_HB
    return pl.pallas_call(
        functools.partial(_sb_kernel, tq=tq, q_start=q_start),
        grid=(bsz, H_B // SB_HEADS, nq),
        in_specs=[pl.BlockSpec((1, group, tq), lambda b, h, i: (b, h, i)),
                  pl.BlockSpec((1, s_len, group), lambda b, h, i: (b, 0, h)),
                  pl.BlockSpec((1, group, s_len), lambda b, h, i: (b, h, 0))],
        out_specs=pl.BlockSpec((tq, group), lambda b, h, i: (b * nq + i, h)),
        out_shape=jax.ShapeDtypeStruct((bsz * t_q, D_B), BF16),
        scratch_shapes=[pltpu.VMEM((2, SB_HEADS, SB_TK, tq), F32), pltpu.VMEM((2, SB_HEADS, SB_TK, tq), BF16),
                        pltpu.VMEM((group, tq), F32)],
        compiler_params=_params(("parallel", "parallel", "arbitrary")),
        name="sb",
    )(qt, k_perm, vt_perm)


def _sb_prep_kernel(ck_ref, cv_ref, nk_ref, nv_ref, kp_ref, vt_ref, *, n_past):
    j = pl.program_id(1)
    perm = _key_permutation(False)
    perm_t = _key_permutation(True)

    @pl.when(j < n_past)
    def _():
        for blk in range(SB_PREP_ROWS // SB_TK):
            cols = slice(blk * SB_TK, (blk + 1) * SB_TK)
            kp_ref[0, cols, :] = lax.dot_general(perm, ck_ref[0, :, cols].astype(BF16), (((1,), (1,)), ((), ())),
                                                 preferred_element_type=F32).astype(BF16)
            vt_ref[0, :, cols] = jnp.dot(cv_ref[0, :, cols].astype(BF16), perm_t,
                                         preferred_element_type=F32).astype(BF16)

    @pl.when(j == n_past)
    def _():
        kp_ref[0, :SB_TK, :] = jnp.dot(perm, nk_ref[0].astype(BF16), preferred_element_type=F32).astype(BF16)
        vt_ref[0, :, :SB_TK] = jnp.dot(nv_ref[0].T.astype(BF16), perm_t, preferred_element_type=F32).astype(BF16)
        kp_ref[0, SB_TK:, :] = jnp.zeros((SB_PREP_ROWS - SB_TK, D_B), BF16)
        vt_ref[0, :, SB_TK:] = jnp.zeros((D_B, SB_PREP_ROWS - SB_TK), BF16)


def _sb_prep(cache_kt, cache_vt, new_k, new_v):
    bsz, _, p_len = cache_kt.shape
    assert p_len % SB_PREP_ROWS == 0
    n_past = p_len // SB_PREP_ROWS
    past = pl.BlockSpec((1, D_B, SB_PREP_ROWS), lambda b, j: (b, 0, jnp.minimum(j, n_past - 1)))
    new = pl.BlockSpec((1, SB_TK, D_B), lambda b, j: (b, 0, 0))
    return pl.pallas_call(
        functools.partial(_sb_prep_kernel, n_past=n_past),
        grid=(bsz, n_past + 1),
        in_specs=[past, past, new, new],
        out_specs=[pl.BlockSpec((1, SB_PREP_ROWS, D_B), lambda b, j: (b, j, 0)),
                   pl.BlockSpec((1, D_B, SB_PREP_ROWS), lambda b, j: (b, 0, j))],
        out_shape=[jax.ShapeDtypeStruct((bsz, p_len + SB_PREP_ROWS, D_B), BF16),
                   jax.ShapeDtypeStruct((bsz, D_B, p_len + SB_PREP_ROWS), BF16)],
        compiler_params=_params(("parallel", "arbitrary")),
        name="sb_prep",
    )(cache_kt, cache_vt, new_k, new_v)


def _mix_ffn_kernel(x_ref, ha_ref, hb_ref, ga_ref, gb_ref, wa_ref, wb_ref, wo_ref, g2_ref, b2_ref,
                    wg_ref, wu_ref, wd_ref, g3_ref, b3_ref, o_ref, *, alpha):
    pa = jnp.dot(ha_ref[...], wa_ref[...], preferred_element_type=F32)
    pb = jnp.dot(hb_ref[...], wb_ref[...], preferred_element_type=F32)
    merged = ga_ref[...] * pa + gb_ref[...] * pb
    mix = jnp.dot(merged.astype(BF16), wo_ref[...], preferred_element_type=F32)
    x2 = _layer_norm(alpha * x_ref[...] + mix, g2_ref[...], b2_ref[...])
    ff = _swiglu(x2.astype(BF16), wg_ref, wu_ref, wd_ref)
    o_ref[...] = _layer_norm(alpha * x2 + 0.5 * ff, g3_ref[...], b3_ref[...])


def _mix_ffn(x1, ha, hb, ga, gb, wa, wb, wo, g2, b2, wg, wu, wd, g3, b3, *, alpha, tm):
    n = x1.shape[0]

    def row(width):
        return pl.BlockSpec((tm, width), lambda i: (i, 0))

    consts = (wa, wb, wo, g2, b2, wg, wu, wd, g3, b3)
    return pl.pallas_call(
        functools.partial(_mix_ffn_kernel, alpha=alpha),
        grid=(n // tm,),
        in_specs=[row(D_MODEL), row(D_AV), row(D_B), row(D_MODEL), row(D_MODEL)]
                 + [_resident(c.shape) for c in consts],
        out_specs=row(D_MODEL),
        out_shape=jax.ShapeDtypeStruct((n, D_MODEL), F32),
        compiler_params=_params(("parallel",)),
        name="mix_ffn",
    )(x1, ha, hb, ga, gb, *consts)


def _channels_first(a):
    bsz, p_len = a.shape[:2]
    return jnp.transpose(a, (0, 2, 3, 1)).reshape(bsz, D_B, p_len)


def _pick_tile(n, cap):
    t = min(n, cap)
    assert n % t == 0
    return t


def _trunk(x, s0, past_k, past_v, w):
    bsz, t_len, _ = x.shape
    n = bsz * t_len
    depth = w["w_in"].shape[0]
    alpha = (2 * depth) ** 0.25
    tm = _pick_tile(n, 512)
    xf = x.reshape(n, D_MODEL)
    ks, vs, ss = [], [], []
    for l in range(depth):
        x1 = _ffn_ln(xf, w["ffn1_wg"][l], w["ffn1_wu"][l], w["ffn1_wd"][l], w["ln1_g"][l], w["ln1_b"][l],
                     alpha=alpha, tm=tm)
        sb_layout = past_k is None
        hq, hk, lf, hv, sg, kb, vb, ga, gb, *sb_ops = _inproj(
            x1, w["w_in"][l], w["lb_logits"], layer=l, tm=tm, t_len=t_len, sb_layout=sb_layout)
        ha, s_new = _hgrn(hq, hk, lf, hv, sg, w["hgrn_norm_g"][l], s0[l],
                          bsz=bsz, t_len=t_len, tb=_pick_tile(t_len, 512))

        if sb_layout:
            qt, k_perm, vt_perm = sb_ops
            hb = _sb(qt, k_perm.reshape(bsz, t_len, D_B), vt_perm, tq=SB_TK, q_start=0)
        else:
            q_start = past_k.shape[2]
            tq = V7X_LANES
            assert t_len <= tq
            pad_new = ((0, 0), (0, SB_TK - t_len), (0, 0))
            k_perm, vt_perm = _sb_prep(_channels_first(past_k[l]), _channels_first(past_v[l]),
                                       jnp.pad(kb.reshape(bsz, t_len, D_B), pad_new),
                                       jnp.pad(vb.reshape(bsz, t_len, D_B), pad_new))
            q3 = jnp.pad(sb_ops[0].reshape(bsz, t_len, D_B), ((0, 0), (0, tq - t_len), (0, 0)))
            hb = _sb(jnp.swapaxes(q3, 1, 2), k_perm, vt_perm, tq=tq, q_start=q_start)
            hb = hb.reshape(bsz, tq, D_B)[:, :t_len].reshape(n, D_B)

        xf = _mix_ffn(x1, ha, hb, ga, gb, w["w_branch_a"][l], w["w_branch_b"][l], w["w_out"][l],
                      w["ln2_g"][l], w["ln2_b"][l], w["ffn2_wg"][l], w["ffn2_wu"][l], w["ffn2_wd"][l],
                      w["ln3_g"][l], w["ln3_b"][l], alpha=alpha, tm=tm)
        if sb_layout:
            kb, vb = (jnp.transpose(a.reshape(bsz, H_B, D_HB, t_len), (0, 3, 1, 2)) for a in (kb, vb))
        ks.append(kb.reshape(bsz, t_len, H_B, D_HB))
        vs.append(vb.reshape(bsz, t_len, H_B, D_HB))
        ss.append(s_new)
    return xf.reshape(bsz, t_len, D_MODEL), jnp.stack(ks), jnp.stack(vs), jnp.stack(ss)


def kernel(x_prompt, x_sample, cache_sb_k, cache_sb_v, state_hgrn, ffn1_wg, ffn1_wu, ffn1_wd, ln1_g, ln1_b,
           w_in, lb_logits, hgrn_norm_g, w_branch_a, w_branch_b, w_out, ln2_g, ln2_b,
           ffn2_wg, ffn2_wu, ffn2_wd, ln3_g, ln3_b):
    depth = w_in.shape[0]

    def vec(p):
        return p[:, None, :]

    w = dict(
        ffn1_wg=ffn1_wg.astype(BF16), ffn1_wu=ffn1_wu.astype(BF16), ffn1_wd=ffn1_wd.astype(BF16),
        ln1_g=vec(ln1_g), ln1_b=vec(ln1_b), w_in=w_in.astype(BF16), lb_logits=lb_logits,
        hgrn_norm_g=vec(hgrn_norm_g), w_branch_a=w_branch_a.astype(BF16), w_branch_b=w_branch_b.astype(BF16),
        w_out=w_out.astype(BF16), ln2_g=vec(ln2_g), ln2_b=vec(ln2_b),
        ffn2_wg=ffn2_wg.astype(BF16), ffn2_wu=ffn2_wu.astype(BF16), ffn2_wd=ffn2_wd.astype(BF16),
        ln3_g=vec(ln3_g), ln3_b=vec(ln3_b))
    s0_prompt = jnp.zeros((depth, x_prompt.shape[0], H_A, K_A, V_A), F32)
    y_p, k_p, v_p, s_p = _trunk(x_prompt, s0_prompt, None, None, w)
    y_s, k_s, v_s, s_s = _trunk(x_sample, state_hgrn, cache_sb_k, cache_sb_v, w)
    return (y_p, y_s, k_p, v_p, s_p, k_s, v_s, s_s)
```

```python
import functools

import jax
import jax.numpy as jnp
from jax import lax
from jax.experimental import pallas as pl
from jax.experimental.pallas import tpu as pltpu

F32 = jnp.float32
BF16 = jnp.bfloat16

D_MODEL = 1024
H_A, K_A, V_A = 4, 128, 128
D_A = H_A * K_A
D_AV = H_A * V_A
H_B, D_HB = 8, 64
D_B = H_B * D_HB
D_FF = 2816
N_IN = 2 * D_A + 2 * D_AV + 3 * D_B + 2 * D_MODEL
LN_EPS = 1e-5
RMS_EPS = 1e-6
SB_SCALE = D_HB ** -0.5

V7X_LANES = 128
V7X_SUBLANES = 8
V7X_MXU_DIM = 256
V7X_VMEM_LIMIT_BYTES = 56 * 1024 * 1024

FF_CHUNK = V7X_MXU_DIM
HGRN_CHUNK = 16
HGRN_BLOCK = 64
SB_TK = 256
SB_NV = SB_TK // V7X_SUBLANES
SB_HEADS = 4
SB_PREP_ROWS = 4 * SB_TK


def _params(sem):
    return pltpu.CompilerParams(dimension_semantics=sem, vmem_limit_bytes=V7X_VMEM_LIMIT_BYTES)


def _resident(shape):
    nd = len(shape)
    return pl.BlockSpec(shape, lambda *_: (0,) * nd, pipeline_mode=pl.Buffered(1))


def _layer_norm(y, g, b):
    mu = jnp.mean(y, axis=-1, keepdims=True)
    d = y - mu
    var = jnp.mean(d * d, axis=-1, keepdims=True)
    return d * lax.rsqrt(var + LN_EPS) * g + b


def _silu(x):
    return x * jax.nn.sigmoid(x)


def _swiglu(xb, wg_ref, wu_ref, wd_ref):
    acc = jnp.zeros((xb.shape[0], D_MODEL), F32)
    for c in range(D_FF // FF_CHUNK):
        sl = slice(c * FF_CHUNK, (c + 1) * FF_CHUNK)
        g = jnp.dot(xb, wg_ref[:, sl], preferred_element_type=F32)
        u = jnp.dot(xb, wu_ref[:, sl], preferred_element_type=F32)
        h = (_silu(g) * u).astype(BF16)
        acc = acc + jnp.dot(h, wd_ref[sl, :], preferred_element_type=F32)
    return acc


def _ffn_ln_kernel(x_ref, wg_ref, wu_ref, wd_ref, g_ref, b_ref, o_ref, *, alpha):
    x = x_ref[...]
    ff = _swiglu(x.astype(BF16), wg_ref, wu_ref, wd_ref)
    o_ref[...] = _layer_norm(alpha * x + 0.5 * ff, g_ref[...], b_ref[...])


def _ffn_ln(x, wg, wu, wd, g, b, *, alpha, tm):
    n = x.shape[0]
    row = pl.BlockSpec((tm, D_MODEL), lambda i: (i, 0))
    return pl.pallas_call(
        functools.partial(_ffn_ln_kernel, alpha=alpha),
        grid=(n // tm,),
        in_specs=[row, _resident(wg.shape), _resident(wu.shape), _resident(wd.shape),
                  _resident(g.shape), _resident(b.shape)],
        out_specs=row,
        out_shape=jax.ShapeDtypeStruct((n, D_MODEL), F32),
        compiler_params=_params(("parallel",)),
        name="ffn_ln",
    )(x, wg, wu, wd, g, b)


_CUTS = (0, D_A, 2 * D_A, 2 * D_A + D_AV, 2 * D_A + 2 * D_AV,
         2 * D_A + 2 * D_AV + D_B, 2 * D_A + 2 * D_AV + 2 * D_B,
         2 * D_A + 2 * D_AV + 3 * D_B, 2 * D_A + 2 * D_AV + 3 * D_B + D_MODEL, N_IN)


def _key_permutation(transposed):
    i = lax.broadcasted_iota(jnp.int32, (SB_TK, SB_TK), 1 if transposed else 0)
    j = lax.broadcasted_iota(jnp.int32, (SB_TK, SB_TK), 0 if transposed else 1)
    src = (i % V7X_SUBLANES) * SB_NV + i // V7X_SUBLANES
    return jnp.where(j == src, 1.0, 0.0).astype(BF16)


def _inproj_kernel(x_ref, w_ref, lbl_ref, hq_ref, hk_ref, lf_ref, hv_ref, sg_ref,
                   kb_ref, vb_ref, ga_ref, gb_ref, *sb_refs, layer, sb_layout):
    xb = x_ref[...].astype(BF16)

    def proj(i):
        return jnp.dot(xb, w_ref[:, _CUTS[i]:_CUTS[i + 1]], preferred_element_type=F32)

    hq_ref[...] = _silu(proj(0))
    rows = [lbl_ref[i:i + 1, :] for i in range(lbl_ref.shape[0])]
    m = functools.reduce(jnp.maximum, rows)
    ex = [jnp.exp(r - m) for r in rows]
    one_minus_lb = sum(ex[layer + 1:]) / sum(ex)
    hk = one_minus_lb * jax.nn.sigmoid(-proj(1))
    hk_ref[...] = hk
    lf_ref[...] = jnp.log1p(-hk)
    hv_ref[...] = proj(2)
    sg_ref[...] = _silu(proj(3))
    ga_ref[...] = jax.nn.sigmoid(proj(7))
    gb_ref[...] = jax.nn.sigmoid(proj(8))
    qh = proj(4) * (0.5 * SB_SCALE)
    kf = proj(5)
    vf = proj(6)
    if not sb_layout:
        kb_ref[...] = kf
        vb_ref[...] = vf
        sb_refs[0][...] = qh.astype(BF16)
        return
    kb_ref[0] = kf.T
    vb_ref[0] = vf.T
    qt_ref, kp_ref, vt_ref = sb_refs
    qt_ref[0] = qh.T.astype(BF16)
    perm = _key_permutation(False)
    perm_t = _key_permutation(True)
    for blk in range(x_ref.shape[0] // SB_TK):
        rows = slice(blk * SB_TK, (blk + 1) * SB_TK)
        kp_ref[rows, :] = jnp.dot(perm, kf[rows].astype(BF16), preferred_element_type=F32).astype(BF16)
        vt_ref[0, :, rows] = jnp.dot(vf[rows].T.astype(BF16), perm_t, preferred_element_type=F32).astype(BF16)


def _inproj(x1, w_in, lb_logits, *, layer, tm, t_len, sb_layout):
    n = x1.shape[0]
    nt = t_len // tm

    def row(width):
        return pl.BlockSpec((tm, width), lambda i: (i, 0))

    cols = pl.BlockSpec((1, D_B, tm), lambda i: (i // nt, 0, i % nt))
    widths = (D_A, D_A, D_A, D_AV, D_AV, D_B, D_B, D_MODEL, D_MODEL)
    specs = [row(w) for w in widths]
    shapes = [jax.ShapeDtypeStruct((n, w), F32) for w in widths]
    if sb_layout:
        assert tm % SB_TK == 0 and t_len % tm == 0
        specs[5:7] = [cols, cols]
        shapes[5:7] = [jax.ShapeDtypeStruct((n // t_len, D_B, t_len), F32)] * 2
        specs += [cols, row(D_B), cols]
        shapes += [jax.ShapeDtypeStruct((n // t_len, D_B, t_len), BF16), jax.ShapeDtypeStruct((n, D_B), BF16),
                   jax.ShapeDtypeStruct((n // t_len, D_B, t_len), BF16)]
    else:
        specs += [row(D_B)]
        shapes += [jax.ShapeDtypeStruct((n, D_B), BF16)]
    return pl.pallas_call(
        functools.partial(_inproj_kernel, layer=layer, sb_layout=sb_layout),
        grid=(n // tm,),
        in_specs=[row(D_MODEL), _resident(w_in.shape), _resident(lb_logits.shape)],
        out_specs=specs,
        out_shape=shapes,
        compiler_params=_params(("parallel",)),
        name="inproj",
    )(x1, w_in, lb_logits)


def _cumsum_rows(x):
    n = x.shape[0]
    row = lax.broadcasted_iota(jnp.int32, x.shape, 0)
    sh = 1
    while sh < n:
        x = x + jnp.where(row >= sh, pltpu.roll(x, sh, axis=0), 0.0)
        sh *= 2
    return x


def _hgrn_kernel(q_ref, k_ref, lf_ref, v_ref, sg_ref, gn_ref, s0_ref, o_ref, sout_ref, st_scr, b_scr, *, tb):
    c = HGRN_CHUNK
    blk = q_ref.shape[1]
    ns = blk // c
    t_blk = pl.program_id(1)

    @pl.when(t_blk == 0)
    def _():
        for h in range(H_A):
            st_scr[h] = s0_ref[0, h].T

    half = V7X_SUBLANES
    row = lax.broadcasted_iota(jnp.int32, (half, K_A), 0)
    ones_kk = jnp.ones((K_A, K_A), BF16)
    nt_dims = (((1,), (1,)), ((), ()))

    def block(i, carry):
        q4 = q_ref[i]
        k4 = k_ref[i]
        v4 = v_ref[i]
        sg4 = sg_ref[i]
        lf4 = lf_ref[i]
        b4 = jnp.concatenate([_cumsum_rows(lf4[j * c:(j + 1) * c]) for j in range(ns)], axis=0)
        b_scr[...] = b4
        outs = [None] * H_A

        def head(h):
            sl = slice(h * K_A, (h + 1) * K_A)
            sub = [slice(j * c, (j + 1) * c) for j in range(ns)]
            q = [q4[r, sl] for r in sub]
            k = [k4[r, sl] for r in sub]
            v = [v4[r, sl] for r in sub]
            b = [b4[r, sl] for r in sub]
            e = [x[c - 1:c, :] for x in b]

            def span(lo, hi):
                return functools.reduce(jnp.add, e[lo:hi]) if hi > lo else None

            def decayed(x, bb, extra):
                return x * jnp.exp(bb if extra is None else bb + extra)

            st = st_scr[h]
            qs = jnp.concatenate([decayed(q[n], b[n], span(0, n)) for n in range(ns)], axis=0)
            o_all = lax.dot_general(qs.astype(BF16), st.astype(BF16), nt_dims, preferred_element_type=F32)
            o = [o_all[r] for r in sub]
            kh = [k[j] * jnp.exp(e[j] - b[j]) for j in range(ns)]
            sc = []
            for j in range(ns - 1):
                qx = jnp.concatenate([decayed(q[n], b[n], span(j + 1, n)) for n in range(j + 1, ns)], axis=0)
                sc.append(lax.dot_general(qx.astype(BF16), kh[j].astype(BF16), nt_dims,
                                          preferred_element_type=F32))
            yield
            for j in range(ns - 1):
                ov = jnp.dot(sc[j].astype(BF16), v[j].astype(BF16), preferred_element_type=F32)
                for m, n in enumerate(range(j + 1, ns)):
                    o[n] = o[n] + ov[m * c:(m + 1) * c]
            yield
            o_h = [[x[:half], x[half:]] for x in o]
            mxu_terms = []
            for j in range(ns):
                for s in range(c):
                    ks = k_ref[i, j * c + s:j * c + s + 1, sl]
                    vs = v_ref[i, j * c + s:j * c + s + 1, sl]
                    bs = b_scr[j * c + s:j * c + s + 1, sl]
                    for p in range(2):
                        lo = p * half
                        if s >= lo + half:
                            continue
                        w = q[j][lo:lo + half] * jnp.exp(b[j][lo:lo + half] - bs) * ks
                        if s > lo:
                            w = jnp.where(row >= s - lo, w, 0.0)
                        if (s + p) % 2:
                            mxu_terms.append((j, p, vs, w))
                        else:
                            o_h[j][p] = o_h[j][p] + jnp.sum(w, axis=-1, keepdims=True) * vs
            sums = jnp.dot(jnp.concatenate([t[3] for t in mxu_terms], axis=0).astype(BF16), ones_kk,
                           preferred_element_type=F32)
            for n, (j, p, vs, _) in enumerate(mxu_terms):
                o_h[j][p] = o_h[j][p] + sums[n * half:(n + 1) * half] * vs
            o = jnp.concatenate([x for pair in o_h for x in pair], axis=0)
            yield
            kd = jnp.concatenate([kh[j] if j == ns - 1 else kh[j] * jnp.exp(span(j + 1, ns))
                                  for j in range(ns)], axis=0)
            vv = jnp.concatenate(v, axis=0)
            upd = lax.dot_general(vv.astype(BF16), kd.astype(BF16), (((0,), (0,)), ((), ())),
                                  preferred_element_type=F32)
            st_scr[h] = st * jnp.exp(span(0, ns)) + upd
            o = o * lax.rsqrt(jnp.mean(o * o, axis=-1, keepdims=True) + RMS_EPS)
            outs[h] = o * gn_ref[:, sl] * sg4[:, sl]
            yield

        heads = [head(h) for h in range(H_A)]
        for _ in range(4):
            for g in heads:
                next(g)
        o_ref[i] = jnp.concatenate(outs, axis=1).astype(o_ref.dtype)
        return carry

    lax.fori_loop(0, tb // blk, block, 0)

    @pl.when(t_blk == pl.num_programs(1) - 1)
    def _():
        for h in range(H_A):
            sout_ref[0, h] = st_scr[h].T


def _hgrn(hq, hk, lf, hv, sg, gn, s0, *, bsz, t_len, tb):
    nt = t_len // tb
    c = min(HGRN_BLOCK, tb)
    assert tb % c == 0 and c % HGRN_CHUNK == 0
    n = bsz * t_len
    row = pl.BlockSpec((tb // c, c, D_A), lambda b, t: (b * nt + t, 0, 0))
    state = pl.BlockSpec((1, H_A, K_A, V_A), lambda b, t: (b, 0, 0, 0))
    ha, s_new = pl.pallas_call(
        functools.partial(_hgrn_kernel, tb=tb),
        grid=(bsz, nt),
        in_specs=[row, row, row, row, row, _resident(gn.shape), state],
        out_specs=[row, state],
        out_shape=[jax.ShapeDtypeStruct((n // c, c, D_AV), BF16),
                   jax.ShapeDtypeStruct((bsz, H_A, K_A, V_A), F32)],
        scratch_shapes=[pltpu.VMEM((H_A, V_A, K_A), F32), pltpu.VMEM((c, D_A), F32)],
        compiler_params=_params(("parallel", "arbitrary")),
        name="hgrn",
    )(*(a.reshape(n // c, c, D_A) for a in (hq, hk, lf, hv, sg)), gn, s0)
    return ha.reshape(n, D_AV), s_new


def _shift_up(x, sh):
    row = lax.broadcasted_iota(jnp.int32, x.shape, 0)
    return jnp.where(row < V7X_SUBLANES - sh, pltpu.roll(x, V7X_SUBLANES - sh, axis=0), 1.0)


def _sb_weights(z_ref, w_ref, carry, *, q_off=None, scale=None):
    tq = z_ref.shape[1]
    new_carry = []
    for c0 in range(0, tq, V7X_LANES):
        cols = slice(c0, c0 + V7X_LANES)
        if q_off is not None:
            sub = lax.broadcasted_iota(jnp.int32, (V7X_SUBLANES, V7X_LANES), 0) * SB_NV
            lane = lax.broadcasted_iota(jnp.int32, (V7X_SUBLANES, V7X_LANES), 1) + (q_off + c0)
        run = jnp.ones((V7X_SUBLANES, V7X_LANES), F32)
        diffs = [None] * SB_NV
        for v in reversed(range(SB_NV)):
            gv = 0.5 - 0.5 * jnp.tanh(z_ref[v * V7X_SUBLANES:(v + 1) * V7X_SUBLANES, cols])
            if q_off is not None:
                gv = jnp.where(sub + v < lane, gv, 1.0)
            nxt = run * gv
            diffs[v] = run - nxt
            run = nxt
        y = _shift_up(run, 1)
        y = y * _shift_up(y, 1)
        y = y * _shift_up(y, 2)
        y = y * _shift_up(y, 4)
        offs = carry[:, cols] * y
        if scale is not None:
            offs = offs * scale
        for v in range(0, SB_NV, 2):
            pair = jnp.concatenate([diffs[v] * offs, diffs[v + 1] * offs], axis=0)
            w_ref[v * V7X_SUBLANES:(v + 2) * V7X_SUBLANES, cols] = pair.astype(BF16)
        new_carry.append(jnp.broadcast_to((offs * run)[0:1, :], run.shape))
    return jnp.concatenate(new_carry, axis=1)


def _sb_kernel(qt_ref, k_ref, vt_ref, o_ref, z_scr, w_scr, acc_scr, *, tq, nqs, q_start):
    heads = range(SB_HEADS)

    def any_left(carries):
        return (jnp.max(functools.reduce(jnp.maximum, carries)) > 0.0).astype(jnp.int32)

    def query_block(u):
        q0 = q_start + (pl.program_id(2) * nqs + u) * tq
        n_full = q0 // SB_TK
        n_pairs = jnp.maximum(n_full, 1) // 2
        lanes = slice(u * tq, (u + 1) * tq)
        qt_all = qt_ref[0, :, lanes]
        head_of_row = lax.broadcasted_iota(jnp.int32, qt_all.shape, 0) // D_HB
        qts = [jnp.where(head_of_row == a, qt_all, jnp.zeros_like(qt_all)) for a in heads]

        def scores(j, slot):
            s0 = pl.multiple_of(jnp.maximum(j, 0) * SB_TK, SB_TK)
            kb = k_ref[0, pl.ds(s0, SB_TK), :]
            for a in heads:
                z_scr[u, slot, a] = jnp.dot(kb, qts[a], preferred_element_type=F32)

        def values(j, slot):
            s0 = pl.multiple_of(jnp.maximum(j, 0) * SB_TK, SB_TK)
            parts = [jnp.dot(vt_ref[0, a * D_HB:(a + 1) * D_HB, pl.ds(s0, SB_TK)], w_scr[u, slot, a],
                             preferred_element_type=F32) for a in heads]
            acc_scr[u] += jnp.concatenate(parts, axis=0)

        def weights(slot, carries, **kw):
            return tuple(_sb_weights(z_scr.at[u, slot, a], w_scr.at[u, slot, a], carries[a], **kw)
                         for a in heads)

        scores(n_full, 1)
        scores(n_full - 1, 0)
        acc_scr[u] = jnp.zeros(acc_scr.shape[1:], F32)
        yield
        ones = jnp.ones((V7X_SUBLANES, tq), F32)
        carries = weights(1, (ones,) * SB_HEADS, q_off=q0 - n_full * SB_TK)
        yield
        values(n_full, 1)
        scores(n_full - 2, 1)
        yield
        carries = weights(0, carries, scale=(n_full >= 1).astype(F32))
        alive = any_left(carries)
        yield

        def pair(state):
            p, carries, _ = state
            j = n_full - 2 - 2 * p
            values(j + 1, 0)
            scores(j - 1, 0)
            carries = weights(1, carries)
            alive = any_left(carries)
            scores(j - 2, 1)
            values(j, 1)
            valid = (j >= 1).astype(F32)
            return p + 1, weights(0, carries, scale=valid), alive

        def more(state):
            return jnp.logical_and(state[0] < n_pairs, state[2] > 0)

        n_done, _, _ = lax.while_loop(more, pair, (jnp.int32(0), carries, alive))
        yield
        values(n_full - 1 - 2 * n_done, 0)
        yield
        o_ref[lanes, :] = acc_scr[u].T.astype(o_ref.dtype)
        yield

    blocks = [query_block(u) for u in range(nqs)]
    for _ in range(7):
        for blk in blocks:
            next(blk)


def _sb(qt, k_perm, vt_perm, *, tq, q_start):
    bsz, _, t_q = qt.shape
    s_len = k_perm.shape[1]
    nq = t_q // tq
    assert tq <= SB_TK and q_start % SB_TK == 0 and SB_TK % tq == 0
    assert s_len % SB_TK == 0 and s_len >= q_start + nq * tq - tq + SB_TK
    group = SB_HEADS * D_HB
    nqs = 2 if nq % 2 == 0 else 1
    steps = nq // nqs
    return pl.pallas_call(
        functools.partial(_sb_kernel, tq=tq, nqs=nqs, q_start=q_start),
        grid=(bsz, H_B // SB_HEADS, steps),
        in_specs=[pl.BlockSpec((1, group, nqs * tq), lambda b, h, i: (b, h, i)),
                  pl.BlockSpec((1, s_len, group), lambda b, h, i: (b, 0, h)),
                  pl.BlockSpec((1, group, s_len), lambda b, h, i: (b, h, 0))],
        out_specs=pl.BlockSpec((nqs * tq, group), lambda b, h, i: (b * steps + i, h)),
        out_shape=jax.ShapeDtypeStruct((bsz * t_q, D_B), BF16),
        scratch_shapes=[pltpu.VMEM((nqs, 2, SB_HEADS, SB_TK, tq), F32),
                        pltpu.VMEM((nqs, 2, SB_HEADS, SB_TK, tq), BF16),
                        pltpu.VMEM((nqs, group, tq), F32)],
        compiler_params=_params(("parallel", "parallel", "arbitrary")),
        name="sb",
    )(qt, k_perm, vt_perm)


def _sb_prep_kernel(ck_ref, cv_ref, nk_ref, nv_ref, kp_ref, vt_ref, *, n_past):
    j = pl.program_id(1)
    perm = _key_permutation(False)
    perm_t = _key_permutation(True)

    @pl.when(j < n_past)
    def _():
        for blk in range(SB_PREP_ROWS // SB_TK):
            cols = slice(blk * SB_TK, (blk + 1) * SB_TK)
            kp_ref[0, cols, :] = lax.dot_general(perm, ck_ref[0, :, cols].astype(BF16), (((1,), (1,)), ((), ())),
                                                 preferred_element_type=F32).astype(BF16)
            vt_ref[0, :, cols] = jnp.dot(cv_ref[0, :, cols].astype(BF16), perm_t,
                                         preferred_element_type=F32).astype(BF16)

    @pl.when(j == n_past)
    def _():
        kp_ref[0, :SB_TK, :] = jnp.dot(perm, nk_ref[0].astype(BF16), preferred_element_type=F32).astype(BF16)
        vt_ref[0, :, :SB_TK] = jnp.dot(nv_ref[0].T.astype(BF16), perm_t, preferred_element_type=F32).astype(BF16)
        kp_ref[0, SB_TK:, :] = jnp.zeros((SB_PREP_ROWS - SB_TK, D_B), BF16)
        vt_ref[0, :, SB_TK:] = jnp.zeros((D_B, SB_PREP_ROWS - SB_TK), BF16)


def _sb_prep(cache_kt, cache_vt, new_k, new_v):
    bsz, _, p_len = cache_kt.shape
    assert p_len % SB_PREP_ROWS == 0
    n_past = p_len // SB_PREP_ROWS
    past = pl.BlockSpec((1, D_B, SB_PREP_ROWS), lambda b, j: (b, 0, jnp.minimum(j, n_past - 1)))
    new = pl.BlockSpec((1, SB_TK, D_B), lambda b, j: (b, 0, 0))
    return pl.pallas_call(
        functools.partial(_sb_prep_kernel, n_past=n_past),
        grid=(bsz, n_past + 1),
        in_specs=[past, past, new, new],
        out_specs=[pl.BlockSpec((1, SB_PREP_ROWS, D_B), lambda b, j: (b, j, 0)),
                   pl.BlockSpec((1, D_B, SB_PREP_ROWS), lambda b, j: (b, 0, j))],
        out_shape=[jax.ShapeDtypeStruct((bsz, p_len + SB_PREP_ROWS, D_B), BF16),
                   jax.ShapeDtypeStruct((bsz, D_B, p_len + SB_PREP_ROWS), BF16)],
        compiler_params=_params(("parallel", "arbitrary")),
        name="sb_prep",
    )(cache_kt, cache_vt, new_k, new_v)


def _mix_ffn_kernel(x_ref, ha_ref, hb_ref, ga_ref, gb_ref, wa_ref, wb_ref, wo_ref, g2_ref, b2_ref,
                    wg_ref, wu_ref, wd_ref, g3_ref, b3_ref, o_ref, *, alpha):
    pa = jnp.dot(ha_ref[...], wa_ref[...], preferred_element_type=F32)
    pb = jnp.dot(hb_ref[...], wb_ref[...], preferred_element_type=F32)
    merged = ga_ref[...] * pa + gb_ref[...] * pb
    mix = jnp.dot(merged.astype(BF16), wo_ref[...], preferred_element_type=F32)
    x2 = _layer_norm(alpha * x_ref[...] + mix, g2_ref[...], b2_ref[...])
    ff = _swiglu(x2.astype(BF16), wg_ref, wu_ref, wd_ref)
    o_ref[...] = _layer_norm(alpha * x2 + 0.5 * ff, g3_ref[...], b3_ref[...])


def _mix_ffn(x1, ha, hb, ga, gb, wa, wb, wo, g2, b2, wg, wu, wd, g3, b3, *, alpha, tm):
    n = x1.shape[0]

    def row(width):
        return pl.BlockSpec((tm, width), lambda i: (i, 0))

    consts = (wa, wb, wo, g2, b2, wg, wu, wd, g3, b3)
    return pl.pallas_call(
        functools.partial(_mix_ffn_kernel, alpha=alpha),
        grid=(n // tm,),
        in_specs=[row(D_MODEL), row(D_AV), row(D_B), row(D_MODEL), row(D_MODEL)]
                 + [_resident(c.shape) for c in consts],
        out_specs=row(D_MODEL),
        out_shape=jax.ShapeDtypeStruct((n, D_MODEL), F32),
        compiler_params=_params(("parallel",)),
        name="mix_ffn",
    )(x1, ha, hb, ga, gb, *consts)


def _channels_first(a):
    bsz, p_len = a.shape[:2]
    return jnp.transpose(a, (0, 2, 3, 1)).reshape(bsz, D_B, p_len)


def _pick_tile(n, cap):
    t = min(n, cap)
    assert n % t == 0
    return t


def _trunk(x, s0, past_k, past_v, w):
    bsz, t_len, _ = x.shape
    n = bsz * t_len
    depth = w["w_in"].shape[0]
    alpha = (2 * depth) ** 0.25
    tm = _pick_tile(n, 512)
    xf = x.reshape(n, D_MODEL)
    ks, vs, ss = [], [], []
    for l in range(depth):
        x1 = _ffn_ln(xf, w["ffn1_wg"][l], w["ffn1_wu"][l], w["ffn1_wd"][l], w["ln1_g"][l], w["ln1_b"][l],
                     alpha=alpha, tm=_pick_tile(n, 1024))
        sb_layout = past_k is None
        hq, hk, lf, hv, sg, kb, vb, ga, gb, *sb_ops = _inproj(
            x1, w["w_in"][l], w["lb_logits"], layer=l, tm=tm, t_len=t_len, sb_layout=sb_layout)
        ha, s_new = _hgrn(hq, hk, lf, hv, sg, w["hgrn_norm_g"][l], s0[l],
                          bsz=bsz, t_len=t_len, tb=_pick_tile(t_len, 1024))

        if sb_layout:
            qt, k_perm, vt_perm = sb_ops
            hb = _sb(qt, k_perm.reshape(bsz, t_len, D_B), vt_perm, tq=SB_TK, q_start=0)
        else:
            q_start = past_k.shape[2]
            tq = V7X_LANES
            assert t_len <= tq
            pad_new = ((0, 0), (0, SB_TK - t_len), (0, 0))
            k_perm, vt_perm = _sb_prep(_channels_first(past_k[l]), _channels_first(past_v[l]),
                                       jnp.pad(kb.reshape(bsz, t_len, D_B), pad_new),
                                       jnp.pad(vb.reshape(bsz, t_len, D_B), pad_new))
            q3 = jnp.pad(sb_ops[0].reshape(bsz, t_len, D_B), ((0, 0), (0, tq - t_len), (0, 0)))
            hb = _sb(jnp.swapaxes(q3, 1, 2), k_perm, vt_perm, tq=tq, q_start=q_start)
            hb = hb.reshape(bsz, tq, D_B)[:, :t_len].reshape(n, D_B)

        xf = _mix_ffn(x1, ha, hb, ga, gb, w["w_branch_a"][l], w["w_branch_b"][l], w["w_out"][l],
                      w["ln2_g"][l], w["ln2_b"][l], w["ffn2_wg"][l], w["ffn2_wu"][l], w["ffn2_wd"][l],
                      w["ln3_g"][l], w["ln3_b"][l], alpha=alpha, tm=tm)
        if sb_layout:
            kb, vb = (jnp.transpose(a.reshape(bsz, H_B, D_HB, t_len), (0, 3, 1, 2)) for a in (kb, vb))
        ks.append(kb.reshape(bsz, t_len, H_B, D_HB))
        vs.append(vb.reshape(bsz, t_len, H_B, D_HB))
        ss.append(s_new)
    return xf.reshape(bsz, t_len, D_MODEL), jnp.stack(ks), jnp.stack(vs), jnp.stack(ss)


def kernel(x_prompt, x_sample, cache_sb_k, cache_sb_v, state_hgrn, ffn1_wg, ffn1_wu, ffn1_wd, ln1_g, ln1_b,
           w_in, lb_logits, hgrn_norm_g, w_branch_a, w_branch_b, w_out, ln2_g, ln2_b,
           ffn2_wg, ffn2_wu, ffn2_wd, ln3_g, ln3_b):
    depth = w_in.shape[0]

    def vec(p):
        return p[:, None, :]

    w = dict(
        ffn1_wg=ffn1_wg.astype(BF16), ffn1_wu=ffn1_wu.astype(BF16), ffn1_wd=ffn1_wd.astype(BF16),
        ln1_g=vec(ln1_g), ln1_b=vec(ln1_b), w_in=w_in.astype(BF16), lb_logits=lb_logits,
        hgrn_norm_g=vec(hgrn_norm_g), w_branch_a=w_branch_a.astype(BF16), w_branch_b=w_branch_b.astype(BF16),
        w_out=w_out.astype(BF16), ln2_g=vec(ln2_g), ln2_b=vec(ln2_b),
        ffn2_wg=ffn2_wg.astype(BF16), ffn2_wu=ffn2_wu.astype(BF16), ffn2_wd=ffn2_wd.astype(BF16),
        ln3_g=vec(ln3_g), ln3_b=vec(ln3_b))
    s0_prompt = jnp.zeros((depth, x_prompt.shape[0], H_A, K_A, V_A), F32)
    y_p, k_p, v_p, s_p = _trunk(x_prompt, s0_prompt, None, None, w)
    y_s, k_s, v_s, s_s = _trunk(x_sample, state_hgrn, cache_sb_k, cache_sb_v, w)
    return (y_p, y_s, k_p, v_p, s_p, k_s, v_s, s_s)
```

```python
import functools

import jax
import jax.numpy as jnp
from jax import lax
from jax.experimental import pallas as pl
from jax.experimental.pallas import tpu as pltpu

F32 = jnp.float32
BF16 = jnp.bfloat16

D_MODEL = 1024
H_A, K_A, V_A = 4, 128, 128
D_A = H_A * K_A
D_AV = H_A * V_A
H_B, D_HB = 8, 64
D_B = H_B * D_HB
D_FF = 2816
N_IN = 2 * D_A + 2 * D_AV + 3 * D_B + 2 * D_MODEL
LN_EPS = 1e-5
RMS_EPS = 1e-6
SB_SCALE = D_HB ** -0.5

V7X_LANES = 128
V7X_SUBLANES = 8
V7X_MXU_DIM = 256
V7X_VMEM_BYTES = 64 * 1024 * 1024
V7X_VMEM_LIMIT_BYTES = V7X_VMEM_BYTES * 7 // 8

FF_CHUNK = V7X_MXU_DIM
HGRN_CHUNK = 2 * V7X_SUBLANES
HGRN_BLOCK = 4 * HGRN_CHUNK
SB_TK = V7X_MXU_DIM
SB_NV = SB_TK // V7X_SUBLANES
SB_HEADS = V7X_MXU_DIM // D_HB
SB_PREP_ROWS = 4 * SB_TK


def _params(sem):
    return pltpu.CompilerParams(dimension_semantics=sem, vmem_limit_bytes=V7X_VMEM_LIMIT_BYTES)


def _resident(shape):
    nd = len(shape)
    return pl.BlockSpec(shape, lambda *_: (0,) * nd, pipeline_mode=pl.Buffered(1))


def _layer_norm(y, g, b):
    mu = jnp.mean(y, axis=-1, keepdims=True)
    d = y - mu
    var = jnp.mean(d * d, axis=-1, keepdims=True)
    return d * lax.rsqrt(var + LN_EPS) * g + b


def _silu(x):
    return x * jax.nn.sigmoid(x)


def _swiglu(xb, wg_ref, wu_ref, wd_ref):
    acc = jnp.zeros((xb.shape[0], D_MODEL), F32)
    for c in range(D_FF // FF_CHUNK):
        sl = slice(c * FF_CHUNK, (c + 1) * FF_CHUNK)
        g = jnp.dot(xb, wg_ref[:, sl], preferred_element_type=F32)
        u = jnp.dot(xb, wu_ref[:, sl], preferred_element_type=F32)
        h = (_silu(g) * u).astype(BF16)
        acc = acc + jnp.dot(h, wd_ref[sl, :], preferred_element_type=F32)
    return acc


def _ffn_ln_kernel(x_ref, wg_ref, wu_ref, wd_ref, g_ref, b_ref, o_ref, *, alpha):
    x = x_ref[...]
    ff = _swiglu(x.astype(BF16), wg_ref, wu_ref, wd_ref)
    o_ref[...] = _layer_norm(alpha * x + 0.5 * ff, g_ref[...], b_ref[...])


def _ffn_ln(x, wg, wu, wd, g, b, *, alpha, tm):
    n = x.shape[0]
    row = pl.BlockSpec((tm, D_MODEL), lambda i: (i, 0))
    return pl.pallas_call(
        functools.partial(_ffn_ln_kernel, alpha=alpha),
        grid=(n // tm,),
        in_specs=[row, _resident(wg.shape), _resident(wu.shape), _resident(wd.shape),
                  _resident(g.shape), _resident(b.shape)],
        out_specs=row,
        out_shape=jax.ShapeDtypeStruct((n, D_MODEL), F32),
        compiler_params=_params(("parallel",)),
        name="ffn_ln",
    )(x, wg, wu, wd, g, b)


_CUTS = (0, D_A, 2 * D_A, 2 * D_A + D_AV, 2 * D_A + 2 * D_AV,
         2 * D_A + 2 * D_AV + D_B, 2 * D_A + 2 * D_AV + 2 * D_B,
         2 * D_A + 2 * D_AV + 3 * D_B, 2 * D_A + 2 * D_AV + 3 * D_B + D_MODEL, N_IN)


def _key_permutation(transposed):
    i = lax.broadcasted_iota(jnp.int32, (SB_TK, SB_TK), 1 if transposed else 0)
    j = lax.broadcasted_iota(jnp.int32, (SB_TK, SB_TK), 0 if transposed else 1)
    src = (i % V7X_SUBLANES) * SB_NV + i // V7X_SUBLANES
    return jnp.where(j == src, 1.0, 0.0).astype(BF16)


def _inproj_kernel(x_ref, w_ref, lbl_ref, hq_ref, hk_ref, lf_ref, hv_ref, sg_ref,
                   kb_ref, vb_ref, ga_ref, gb_ref, *sb_refs, layer, sb_layout):
    xb = x_ref[...].astype(BF16)

    def proj(i):
        return jnp.dot(xb, w_ref[:, _CUTS[i]:_CUTS[i + 1]], preferred_element_type=F32)

    hq_ref[...] = _silu(proj(0))
    rows = [lbl_ref[i:i + 1, :] for i in range(lbl_ref.shape[0])]
    m = functools.reduce(jnp.maximum, rows)
    ex = [jnp.exp(r - m) for r in rows]
    one_minus_lb = sum(ex[layer + 1:]) / sum(ex)
    hk = one_minus_lb * jax.nn.sigmoid(-proj(1))
    hk_ref[...] = hk
    lf_ref[...] = jnp.log1p(-hk)
    hv_ref[...] = proj(2)
    sg_ref[...] = _silu(proj(3))
    ga_ref[...] = jax.nn.sigmoid(proj(7))
    gb_ref[...] = jax.nn.sigmoid(proj(8))
    qh = proj(4) * (0.5 * SB_SCALE)
    kf = proj(5)
    vf = proj(6)
    if not sb_layout:
        kb_ref[...] = kf
        vb_ref[...] = vf
        sb_refs[0][...] = qh.astype(BF16)
        return
    kb_ref[0] = kf.T
    vb_ref[0] = vf.T
    qt_ref, kp_ref, vt_ref = sb_refs
    qt_ref[0] = qh.T.astype(BF16)
    perm = _key_permutation(False)
    perm_t = _key_permutation(True)
    for blk in range(x_ref.shape[0] // SB_TK):
        rows = slice(blk * SB_TK, (blk + 1) * SB_TK)
        kp_ref[rows, :] = jnp.dot(perm, kf[rows].astype(BF16), preferred_element_type=F32).astype(BF16)
        vt_ref[0, :, rows] = jnp.dot(vf[rows].T.astype(BF16), perm_t, preferred_element_type=F32).astype(BF16)


def _inproj(x1, w_in, lb_logits, *, layer, tm, t_len, sb_layout):
    n = x1.shape[0]
    nt = t_len // tm

    def row(width):
        return pl.BlockSpec((tm, width), lambda i: (i, 0))

    cols = pl.BlockSpec((1, D_B, tm), lambda i: (i // nt, 0, i % nt))
    widths = (D_A, D_A, D_A, D_AV, D_AV, D_B, D_B, D_MODEL, D_MODEL)
    specs = [row(w) for w in widths]
    shapes = [jax.ShapeDtypeStruct((n, w), F32) for w in widths]
    if sb_layout:
        assert tm % SB_TK == 0 and t_len % tm == 0
        specs[5:7] = [cols, cols]
        shapes[5:7] = [jax.ShapeDtypeStruct((n // t_len, D_B, t_len), F32)] * 2
        specs += [cols, row(D_B), cols]
        shapes += [jax.ShapeDtypeStruct((n // t_len, D_B, t_len), BF16), jax.ShapeDtypeStruct((n, D_B), BF16),
                   jax.ShapeDtypeStruct((n // t_len, D_B, t_len), BF16)]
    else:
        specs += [row(D_B)]
        shapes += [jax.ShapeDtypeStruct((n, D_B), BF16)]
    return pl.pallas_call(
        functools.partial(_inproj_kernel, layer=layer, sb_layout=sb_layout),
        grid=(n // tm,),
        in_specs=[row(D_MODEL), _resident(w_in.shape), _resident(lb_logits.shape)],
        out_specs=specs,
        out_shape=shapes,
        compiler_params=_params(("parallel",)),
        name="inproj",
    )(x1, w_in, lb_logits)


def _cumsum_rows(x):
    n = x.shape[0]
    row = lax.broadcasted_iota(jnp.int32, x.shape, 0)
    sh = 1
    while sh < n:
        x = x + jnp.where(row >= sh, pltpu.roll(x, sh, axis=0), 0.0)
        sh *= 2
    return x


def _hgrn_kernel(q_ref, k_ref, lf_ref, v_ref, sg_ref, gn_ref, s0_ref, o_ref, sout_ref, st_scr, b_scr, *, tb):
    c = HGRN_CHUNK
    blk = q_ref.shape[1]
    ns = blk // c
    t_blk = pl.program_id(1)

    @pl.when(t_blk == 0)
    def _():
        for h in range(H_A):
            st_scr[h] = s0_ref[0, h].T

    half = V7X_SUBLANES
    row = lax.broadcasted_iota(jnp.int32, (half, K_A), 0)
    ones_kk = jnp.ones((K_A, K_A), BF16)
    nt_dims = (((1,), (1,)), ((), ()))

    def block(i, carry):
        q4 = q_ref[i]
        k4 = k_ref[i]
        v4 = v_ref[i]
        sg4 = sg_ref[i]
        lf4 = lf_ref[i]
        b4 = jnp.concatenate([_cumsum_rows(lf4[j * c:(j + 1) * c]) for j in range(ns)], axis=0)
        b_scr[...] = b4
        outs = [None] * H_A

        def head(h):
            sl = slice(h * K_A, (h + 1) * K_A)
            sub = [slice(j * c, (j + 1) * c) for j in range(ns)]
            q = [q4[r, sl] for r in sub]
            k = [k4[r, sl] for r in sub]
            v = [v4[r, sl] for r in sub]
            b = [b4[r, sl] for r in sub]
            e = [x[c - 1:c, :] for x in b]

            def span(lo, hi):
                return functools.reduce(jnp.add, e[lo:hi]) if hi > lo else None

            def decayed(x, bb, extra):
                return x * jnp.exp(bb if extra is None else bb + extra)

            st = st_scr[h]
            qs = jnp.concatenate([decayed(q[n], b[n], span(0, n)) for n in range(ns)], axis=0)
            o_all = lax.dot_general(qs.astype(BF16), st.astype(BF16), nt_dims, preferred_element_type=F32)
            o = [o_all[r] for r in sub]
            kh = [k[j] * jnp.exp(e[j] - b[j]) for j in range(ns)]
            sc = []
            for j in range(ns - 1):
                qx = jnp.concatenate([decayed(q[n], b[n], span(j + 1, n)) for n in range(j + 1, ns)], axis=0)
                sc.append(lax.dot_general(qx.astype(BF16), kh[j].astype(BF16), nt_dims,
                                          preferred_element_type=F32))
            yield
            for j in range(ns - 1):
                ov = jnp.dot(sc[j].astype(BF16), v[j].astype(BF16), preferred_element_type=F32)
                for m, n in enumerate(range(j + 1, ns)):
                    o[n] = o[n] + ov[m * c:(m + 1) * c]
            yield
            o_h = [[x[:half], x[half:]] for x in o]
            mxu_terms = []
            for j in range(ns):
                for s in range(c):
                    ks = k_ref[i, j * c + s:j * c + s + 1, sl]
                    vs = v_ref[i, j * c + s:j * c + s + 1, sl]
                    bs = b_scr[j * c + s:j * c + s + 1, sl]
                    for p in range(2):
                        lo = p * half
                        if s >= lo + half:
                            continue
                        w = q[j][lo:lo + half] * jnp.exp(b[j][lo:lo + half] - bs) * ks
                        if s > lo:
                            w = jnp.where(row >= s - lo, w, 0.0)
                        if (s + p) % 2:
                            mxu_terms.append((j, p, vs, w))
                        else:
                            o_h[j][p] = o_h[j][p] + jnp.sum(w, axis=-1, keepdims=True) * vs
            sums = jnp.dot(jnp.concatenate([t[3] for t in mxu_terms], axis=0).astype(BF16), ones_kk,
                           preferred_element_type=F32)
            for n, (j, p, vs, _) in enumerate(mxu_terms):
                o_h[j][p] = o_h[j][p] + sums[n * half:(n + 1) * half] * vs
            o = jnp.concatenate([x for pair in o_h for x in pair], axis=0)
            yield
            kd = jnp.concatenate([kh[j] if j == ns - 1 else kh[j] * jnp.exp(span(j + 1, ns))
                                  for j in range(ns)], axis=0)
            vv = jnp.concatenate(v, axis=0)
            upd = lax.dot_general(vv.astype(BF16), kd.astype(BF16), (((0,), (0,)), ((), ())),
                                  preferred_element_type=F32)
            st_scr[h] = st * jnp.exp(span(0, ns)) + upd
            o = o * lax.rsqrt(jnp.mean(o * o, axis=-1, keepdims=True) + RMS_EPS)
            outs[h] = o * gn_ref[:, sl] * sg4[:, sl]
            yield

        heads = [head(h) for h in range(H_A)]
        for _ in range(4):
            for g in heads:
                next(g)
        o_ref[i] = jnp.concatenate(outs, axis=1).astype(o_ref.dtype)
        return carry

    lax.fori_loop(0, tb // blk, block, 0)

    @pl.when(t_blk == pl.num_programs(1) - 1)
    def _():
        for h in range(H_A):
            sout_ref[0, h] = st_scr[h].T


def _hgrn(hq, hk, lf, hv, sg, gn, s0, *, bsz, t_len, tb):
    nt = t_len // tb
    c = min(HGRN_BLOCK, tb)
    assert tb % c == 0 and c % HGRN_CHUNK == 0
    n = bsz * t_len
    row = pl.BlockSpec((tb // c, c, D_A), lambda b, t: (b * nt + t, 0, 0))
    state = pl.BlockSpec((1, H_A, K_A, V_A), lambda b, t: (b, 0, 0, 0))
    ha, s_new = pl.pallas_call(
        functools.partial(_hgrn_kernel, tb=tb),
        grid=(bsz, nt),
        in_specs=[row, row, row, row, row, _resident(gn.shape), state],
        out_specs=[row, state],
        out_shape=[jax.ShapeDtypeStruct((n // c, c, D_AV), BF16),
                   jax.ShapeDtypeStruct((bsz, H_A, K_A, V_A), F32)],
        scratch_shapes=[pltpu.VMEM((H_A, V_A, K_A), F32), pltpu.VMEM((c, D_A), F32)],
        compiler_params=_params(("parallel", "arbitrary")),
        name="hgrn",
    )(*(a.reshape(n // c, c, D_A) for a in (hq, hk, lf, hv, sg)), gn, s0)
    return ha.reshape(n, D_AV), s_new


def _shift_up(x, sh):
    row = lax.broadcasted_iota(jnp.int32, x.shape, 0)
    return jnp.where(row < V7X_SUBLANES - sh, pltpu.roll(x, V7X_SUBLANES - sh, axis=0), 1.0)


def _sb_weights(z_ref, w_ref, carry, *, q_off=None, scale=None):
    tq = z_ref.shape[1]
    new_carry = []
    for c0 in range(0, tq, V7X_LANES):
        cols = slice(c0, c0 + V7X_LANES)
        if q_off is not None:
            sub = lax.broadcasted_iota(jnp.int32, (V7X_SUBLANES, V7X_LANES), 0) * SB_NV
            lane = lax.broadcasted_iota(jnp.int32, (V7X_SUBLANES, V7X_LANES), 1) + (q_off + c0)
        run = jnp.ones((V7X_SUBLANES, V7X_LANES), F32)
        diffs = [None] * SB_NV
        for v in reversed(range(SB_NV)):
            gv = 0.5 - 0.5 * jnp.tanh(z_ref[v * V7X_SUBLANES:(v + 1) * V7X_SUBLANES, cols])
            if q_off is not None:
                gv = jnp.where(sub + v < lane, gv, 1.0)
            nxt = run * gv
            diffs[v] = run - nxt
            run = nxt
        y = _shift_up(run, 1)
        y = y * _shift_up(y, 1)
        y = y * _shift_up(y, 2)
        y = y * _shift_up(y, 4)
        offs = carry[:, cols] * y
        if scale is not None:
            offs = offs * scale
        for v in range(0, SB_NV, 2):
            pair = jnp.concatenate([diffs[v] * offs, diffs[v + 1] * offs], axis=0)
            w_ref[v * V7X_SUBLANES:(v + 2) * V7X_SUBLANES, cols] = pair.astype(BF16)
        new_carry.append(jnp.broadcast_to((offs * run)[0:1, :], run.shape))
    return jnp.concatenate(new_carry, axis=1)


def _sb_kernel(qt_ref, k_ref, vt_ref, o_ref, z_scr, w_scr, acc_scr, *, tq, nqs, q_start):
    heads = range(SB_HEADS)

    def any_left(carries):
        return (jnp.max(functools.reduce(jnp.maximum, carries)) > 0.0).astype(jnp.int32)

    def query_block(u):
        q0 = q_start + (pl.program_id(2) * nqs + u) * tq
        n_full = q0 // SB_TK
        n_pairs = jnp.maximum(n_full, 1) // 2
        lanes = slice(u * tq, (u + 1) * tq)
        qt_all = qt_ref[0, :, lanes]
        head_of_row = lax.broadcasted_iota(jnp.int32, qt_all.shape, 0) // D_HB
        qts = [jnp.where(head_of_row == a, qt_all, jnp.zeros_like(qt_all)) for a in heads]

        def scores(j, slot):
            s0 = pl.multiple_of(jnp.maximum(j, 0) * SB_TK, SB_TK)
            kb = k_ref[0, pl.ds(s0, SB_TK), :]
            for a in heads:
                z_scr[u, slot, a] = jnp.dot(kb, qts[a], preferred_element_type=F32)

        def values(j, slot):
            s0 = pl.multiple_of(jnp.maximum(j, 0) * SB_TK, SB_TK)
            parts = [jnp.dot(vt_ref[0, a * D_HB:(a + 1) * D_HB, pl.ds(s0, SB_TK)], w_scr[u, slot, a],
                             preferred_element_type=F32) for a in heads]
            acc_scr[u] += jnp.concatenate(parts, axis=0)

        def weights(slot, carries, **kw):
            return tuple(_sb_weights(z_scr.at[u, slot, a], w_scr.at[u, slot, a], carries[a], **kw)
                         for a in heads)

        scores(n_full, 1)
        scores(n_full - 1, 0)
        acc_scr[u] = jnp.zeros(acc_scr.shape[1:], F32)
        yield
        ones = jnp.ones((V7X_SUBLANES, tq), F32)
        carries = weights(1, (ones,) * SB_HEADS, q_off=q0 - n_full * SB_TK)
        yield
        values(n_full, 1)
        scores(n_full - 2, 1)
        yield
        carries = weights(0, carries, scale=(n_full >= 1).astype(F32))
        alive = any_left(carries)
        yield

        def pair(state):
            p, carries, _ = state
            j = n_full - 2 - 2 * p
            values(j + 1, 0)
            scores(j - 1, 0)
            carries = weights(1, carries)
            alive = any_left(carries)
            scores(j - 2, 1)
            values(j, 1)
            valid = (j >= 1).astype(F32)
            return p + 1, weights(0, carries, scale=valid), alive

        def more(state):
            return jnp.logical_and(state[0] < n_pairs, state[2] > 0)

        n_done, _, _ = lax.while_loop(more, pair, (jnp.int32(0), carries, alive))
        yield
        values(n_full - 1 - 2 * n_done, 0)
        yield
        o_ref[lanes, :] = acc_scr[u].T.astype(o_ref.dtype)
        yield

    blocks = [query_block(u) for u in range(nqs)]
    for _ in range(7):
        for blk in blocks:
            next(blk)


def _sb(qt, k_perm, vt_perm, *, tq, q_start):
    bsz, _, t_q = qt.shape
    s_len = k_perm.shape[1]
    nq = t_q // tq
    assert tq <= SB_TK and q_start % SB_TK == 0 and SB_TK % tq == 0
    assert s_len % SB_TK == 0 and s_len >= q_start + nq * tq - tq + SB_TK
    group = SB_HEADS * D_HB
    nqs = 2 if nq % 2 == 0 else 1
    steps = nq // nqs
    return pl.pallas_call(
        functools.partial(_sb_kernel, tq=tq, nqs=nqs, q_start=q_start),
        grid=(bsz, H_B // SB_HEADS, steps),
        in_specs=[pl.BlockSpec((1, group, nqs * tq), lambda b, h, i: (b, h, i)),
                  pl.BlockSpec((1, s_len, group), lambda b, h, i: (b, 0, h)),
                  pl.BlockSpec((1, group, s_len), lambda b, h, i: (b, h, 0))],
        out_specs=pl.BlockSpec((nqs * tq, group), lambda b, h, i: (b * steps + i, h)),
        out_shape=jax.ShapeDtypeStruct((bsz * t_q, D_B), BF16),
        scratch_shapes=[pltpu.VMEM((nqs, 2, SB_HEADS, SB_TK, tq), F32),
                        pltpu.VMEM((nqs, 2, SB_HEADS, SB_TK, tq), BF16),
                        pltpu.VMEM((nqs, group, tq), F32)],
        compiler_params=_params(("parallel", "parallel", "arbitrary")),
        name="sb",
    )(qt, k_perm, vt_perm)


def _sb_prep_kernel(ck_ref, cv_ref, nk_ref, nv_ref, kp_ref, vt_ref, *, n_past):
    j = pl.program_id(1)
    perm = _key_permutation(False)
    perm_t = _key_permutation(True)

    @pl.when(j < n_past)
    def _():
        for blk in range(SB_PREP_ROWS // SB_TK):
            cols = slice(blk * SB_TK, (blk + 1) * SB_TK)
            kp_ref[0, cols, :] = lax.dot_general(perm, ck_ref[0, :, cols].astype(BF16), (((1,), (1,)), ((), ())),
                                                 preferred_element_type=F32).astype(BF16)
            vt_ref[0, :, cols] = jnp.dot(cv_ref[0, :, cols].astype(BF16), perm_t,
                                         preferred_element_type=F32).astype(BF16)

    @pl.when(j == n_past)
    def _():
        kp_ref[0, :SB_TK, :] = jnp.dot(perm, nk_ref[0].astype(BF16), preferred_element_type=F32).astype(BF16)
        vt_ref[0, :, :SB_TK] = jnp.dot(nv_ref[0].T.astype(BF16), perm_t, preferred_element_type=F32).astype(BF16)
        kp_ref[0, SB_TK:, :] = jnp.zeros((SB_PREP_ROWS - SB_TK, D_B), BF16)
        vt_ref[0, :, SB_TK:] = jnp.zeros((D_B, SB_PREP_ROWS - SB_TK), BF16)


def _sb_prep(cache_kt, cache_vt, new_k, new_v):
    bsz, _, p_len = cache_kt.shape
    assert p_len % SB_PREP_ROWS == 0
    n_past = p_len // SB_PREP_ROWS
    past = pl.BlockSpec((1, D_B, SB_PREP_ROWS), lambda b, j: (b, 0, jnp.minimum(j, n_past - 1)))
    new = pl.BlockSpec((1, SB_TK, D_B), lambda b, j: (b, 0, 0))
    return pl.pallas_call(
        functools.partial(_sb_prep_kernel, n_past=n_past),
        grid=(bsz, n_past + 1),
        in_specs=[past, past, new, new],
        out_specs=[pl.BlockSpec((1, SB_PREP_ROWS, D_B), lambda b, j: (b, j, 0)),
                   pl.BlockSpec((1, D_B, SB_PREP_ROWS), lambda b, j: (b, 0, j))],
        out_shape=[jax.ShapeDtypeStruct((bsz, p_len + SB_PREP_ROWS, D_B), BF16),
                   jax.ShapeDtypeStruct((bsz, D_B, p_len + SB_PREP_ROWS), BF16)],
        compiler_params=_params(("parallel", "arbitrary")),
        name="sb_prep",
    )(cache_kt, cache_vt, new_k, new_v)


def _mix_ffn_kernel(x_ref, ha_ref, hb_ref, ga_ref, gb_ref, wa_ref, wb_ref, wo_ref, g2_ref, b2_ref,
                    wg_ref, wu_ref, wd_ref, g3_ref, b3_ref, o_ref, *, alpha):
    pa = jnp.dot(ha_ref[...], wa_ref[...], preferred_element_type=F32)
    pb = jnp.dot(hb_ref[...], wb_ref[...], preferred_element_type=F32)
    merged = ga_ref[...] * pa + gb_ref[...] * pb
    mix = jnp.dot(merged.astype(BF16), wo_ref[...], preferred_element_type=F32)
    x2 = _layer_norm(alpha * x_ref[...] + mix, g2_ref[...], b2_ref[...])
    ff = _swiglu(x2.astype(BF16), wg_ref, wu_ref, wd_ref)
    o_ref[...] = _layer_norm(alpha * x2 + 0.5 * ff, g3_ref[...], b3_ref[...])


def _mix_ffn(x1, ha, hb, ga, gb, wa, wb, wo, g2, b2, wg, wu, wd, g3, b3, *, alpha, tm):
    n = x1.shape[0]

    def row(width):
        return pl.BlockSpec((tm, width), lambda i: (i, 0))

    consts = (wa, wb, wo, g2, b2, wg, wu, wd, g3, b3)
    return pl.pallas_call(
        functools.partial(_mix_ffn_kernel, alpha=alpha),
        grid=(n // tm,),
        in_specs=[row(D_MODEL), row(D_AV), row(D_B), row(D_MODEL), row(D_MODEL)]
                 + [_resident(c.shape) for c in consts],
        out_specs=row(D_MODEL),
        out_shape=jax.ShapeDtypeStruct((n, D_MODEL), F32),
        compiler_params=_params(("parallel",)),
        name="mix_ffn",
    )(x1, ha, hb, ga, gb, *consts)


def _channels_first(a):
    bsz, p_len = a.shape[:2]
    return jnp.transpose(a, (0, 2, 3, 1)).reshape(bsz, D_B, p_len)


def _pick_tile(n, cap):
    t = min(n, cap)
    assert n % t == 0
    return t


def _trunk(x, s0, past_k, past_v, w):
    bsz, t_len, _ = x.shape
    n = bsz * t_len
    depth = w["w_in"].shape[0]
    alpha = (2 * depth) ** 0.25
    tm = _pick_tile(n, 512)
    xf = x.reshape(n, D_MODEL)
    ks, vs, ss = [], [], []
    for l in range(depth):
        x1 = _ffn_ln(xf, w["ffn1_wg"][l], w["ffn1_wu"][l], w["ffn1_wd"][l], w["ln1_g"][l], w["ln1_b"][l],
                     alpha=alpha, tm=_pick_tile(n, 1024))
        sb_layout = past_k is None
        hq, hk, lf, hv, sg, kb, vb, ga, gb, *sb_ops = _inproj(
            x1, w["w_in"][l], w["lb_logits"], layer=l, tm=tm, t_len=t_len, sb_layout=sb_layout)
        ha, s_new = _hgrn(hq, hk, lf, hv, sg, w["hgrn_norm_g"][l], s0[l],
                          bsz=bsz, t_len=t_len, tb=_pick_tile(t_len, 1024))

        if sb_layout:
            qt, k_perm, vt_perm = sb_ops
            hb = _sb(qt, k_perm.reshape(bsz, t_len, D_B), vt_perm, tq=SB_TK, q_start=0)
        else:
            q_start = past_k.shape[2]
            tq = V7X_LANES
            assert t_len <= tq
            pad_new = ((0, 0), (0, SB_TK - t_len), (0, 0))
            k_perm, vt_perm = _sb_prep(_channels_first(past_k[l]), _channels_first(past_v[l]),
                                       jnp.pad(kb.reshape(bsz, t_len, D_B), pad_new),
                                       jnp.pad(vb.reshape(bsz, t_len, D_B), pad_new))
            q3 = jnp.pad(sb_ops[0].reshape(bsz, t_len, D_B), ((0, 0), (0, tq - t_len), (0, 0)))
            hb = _sb(jnp.swapaxes(q3, 1, 2), k_perm, vt_perm, tq=tq, q_start=q_start)
            hb = hb.reshape(bsz, tq, D_B)[:, :t_len].reshape(n, D_B)

        xf = _mix_ffn(x1, ha, hb, ga, gb, w["w_branch_a"][l], w["w_branch_b"][l], w["w_out"][l],
                      w["ln2_g"][l], w["ln2_b"][l], w["ffn2_wg"][l], w["ffn2_wu"][l], w["ffn2_wd"][l],
                      w["ln3_g"][l], w["ln3_b"][l], alpha=alpha, tm=tm)
        if sb_layout:
            kb, vb = (jnp.transpose(a.reshape(bsz, H_B, D_HB, t_len), (0, 3, 1, 2)) for a in (kb, vb))
        ks.append(kb.reshape(bsz, t_len, H_B, D_HB))
        vs.append(vb.reshape(bsz, t_len, H_B, D_HB))
        ss.append(s_new)
    return xf.reshape(bsz, t_len, D_MODEL), jnp.stack(ks), jnp.stack(vs), jnp.stack(ss)


def kernel(x_prompt, x_sample, cache_sb_k, cache_sb_v, state_hgrn, ffn1_wg, ffn1_wu, ffn1_wd, ln1_g, ln1_b,
           w_in, lb_logits, hgrn_norm_g, w_branch_a, w_branch_b, w_out, ln2_g, ln2_b,
           ffn2_wg, ffn2_wu, ffn2_wd, ln3_g, ln3_b):
    depth = w_in.shape[0]

    def vec(p):
        return p[:, None, :]

    w = dict(
        ffn1_wg=ffn1_wg.astype(BF16), ffn1_wu=ffn1_wu.astype(BF16), ffn1_wd=ffn1_wd.astype(BF16),
        ln1_g=vec(ln1_g), ln1_b=vec(ln1_b), w_in=w_in.astype(BF16), lb_logits=lb_logits,
        hgrn_norm_g=vec(hgrn_norm_g), w_branch_a=w_branch_a.astype(BF16), w_branch_b=w_branch_b.astype(BF16),
        w_out=w_out.astype(BF16), ln2_g=vec(ln2_g), ln2_b=vec(ln2_b),
        ffn2_wg=ffn2_wg.astype(BF16), ffn2_wu=ffn2_wu.astype(BF16), ffn2_wd=ffn2_wd.astype(BF16),
        ln3_g=vec(ln3_g), ln3_b=vec(ln3_b))
    s0_prompt = jnp.zeros((depth, x_prompt.shape[0], H_A, K_A, V_A), F32)
    y_p, k_p, v_p, s_p = _trunk(x_prompt, s0_prompt, None, None, w)
    y_s, k_s, v_s, s_s = _trunk(x_sample, state_hgrn, cache_sb_k, cache_sb_v, w)
    return (y_p, y_s, k_p, v_p, s_p, k_s, v_s, s_s)
```

```python
import functools

import jax
import jax.numpy as jnp
from jax import lax
from jax.experimental import pallas as pl
from jax.experimental.pallas import tpu as pltpu

F32 = jnp.float32
BF16 = jnp.bfloat16

D_MODEL = 1024
H_A, K_A, V_A = 4, 128, 128
D_A = H_A * K_A
D_AV = H_A * V_A
H_B, D_HB = 8, 64
D_B = H_B * D_HB
D_FF = 2816
N_IN = 2 * D_A + 2 * D_AV + 3 * D_B + 2 * D_MODEL
LN_EPS = 1e-5
RMS_EPS = 1e-6
SB_SCALE = D_HB ** -0.5

V7X_LANES = 128
V7X_SUBLANES = 8
V7X_MXU_DIM = 256
V7X_VMEM_BYTES = 64 * 1024 * 1024
V7X_VMEM_LIMIT_BYTES = V7X_VMEM_BYTES * 7 // 8

FF_CHUNK = V7X_MXU_DIM
HGRN_CHUNK = 2 * V7X_SUBLANES
HGRN_BLOCK = 4 * HGRN_CHUNK
SB_TK = V7X_MXU_DIM
SB_NV = SB_TK // V7X_SUBLANES
SB_HEADS = V7X_MXU_DIM // D_HB
SB_PREP_ROWS = 4 * SB_TK


def _params(sem):
    return pltpu.CompilerParams(dimension_semantics=sem, vmem_limit_bytes=V7X_VMEM_LIMIT_BYTES)


def _resident(shape):
    nd = len(shape)
    return pl.BlockSpec(shape, lambda *_: (0,) * nd, pipeline_mode=pl.Buffered(1))


def _layer_norm(y, g, b):
    mu = jnp.mean(y, axis=-1, keepdims=True)
    d = y - mu
    var = jnp.mean(d * d, axis=-1, keepdims=True)
    return d * lax.rsqrt(var + LN_EPS) * g + b


def _silu(x):
    return x * jax.nn.sigmoid(x)


def _swiglu(xb, wg_ref, wu_ref, wd_ref):
    acc = jnp.zeros((xb.shape[0], D_MODEL), F32)
    for c in range(D_FF // FF_CHUNK):
        sl = slice(c * FF_CHUNK, (c + 1) * FF_CHUNK)
        g = jnp.dot(xb, wg_ref[:, sl], preferred_element_type=F32)
        u = jnp.dot(xb, wu_ref[:, sl], preferred_element_type=F32)
        h = (_silu(g) * u).astype(BF16)
        acc = acc + jnp.dot(h, wd_ref[sl, :], preferred_element_type=F32)
    return acc


def _ffn_ln_kernel(x_ref, wg_ref, wu_ref, wd_ref, g_ref, b_ref, o_ref, *, alpha):
    x = x_ref[...]
    ff = _swiglu(x.astype(BF16), wg_ref, wu_ref, wd_ref)
    o_ref[...] = _layer_norm(alpha * x + 0.5 * ff, g_ref[...], b_ref[...])


def _ffn_ln(x, wg, wu, wd, g, b, *, alpha, tm):
    n = x.shape[0]
    row = pl.BlockSpec((tm, D_MODEL), lambda i: (i, 0))
    return pl.pallas_call(
        functools.partial(_ffn_ln_kernel, alpha=alpha),
        grid=(n // tm,),
        in_specs=[row, _resident(wg.shape), _resident(wu.shape), _resident(wd.shape),
                  _resident(g.shape), _resident(b.shape)],
        out_specs=row,
        out_shape=jax.ShapeDtypeStruct((n, D_MODEL), F32),
        compiler_params=_params(("parallel",)),
        name="ffn_ln",
    )(x, wg, wu, wd, g, b)


_CUTS = (0, D_A, 2 * D_A, 2 * D_A + D_AV, 2 * D_A + 2 * D_AV,
         2 * D_A + 2 * D_AV + D_B, 2 * D_A + 2 * D_AV + 2 * D_B,
         2 * D_A + 2 * D_AV + 3 * D_B, 2 * D_A + 2 * D_AV + 3 * D_B + D_MODEL, N_IN)


def _key_permutation(transposed):
    i = lax.broadcasted_iota(jnp.int32, (SB_TK, SB_TK), 1 if transposed else 0)
    j = lax.broadcasted_iota(jnp.int32, (SB_TK, SB_TK), 0 if transposed else 1)
    src = (i % V7X_SUBLANES) * SB_NV + i // V7X_SUBLANES
    return jnp.where(j == src, 1.0, 0.0).astype(BF16)


def _inproj_kernel(x_ref, w_ref, lbl_ref, hq_ref, hk_ref, lf_ref, hv_ref, sg_ref,
                   kb_ref, vb_ref, ga_ref, gb_ref, *sb_refs, layer, sb_layout):
    xb = x_ref[...].astype(BF16)

    def proj(i):
        return jnp.dot(xb, w_ref[:, _CUTS[i]:_CUTS[i + 1]], preferred_element_type=F32)

    hq_ref[...] = _silu(proj(0))
    rows = [lbl_ref[i:i + 1, :] for i in range(lbl_ref.shape[0])]
    m = functools.reduce(jnp.maximum, rows)
    ex = [jnp.exp(r - m) for r in rows]
    one_minus_lb = sum(ex[layer + 1:]) / sum(ex)
    hk = one_minus_lb * jax.nn.sigmoid(-proj(1))
    hk_ref[...] = hk
    lf_ref[...] = jnp.log1p(-hk)
    hv_ref[...] = proj(2)
    sg_ref[...] = _silu(proj(3))
    ga_ref[...] = jax.nn.sigmoid(proj(7))
    gb_ref[...] = jax.nn.sigmoid(proj(8))
    qh = proj(4) * (0.5 * SB_SCALE)
    kf = proj(5)
    vf = proj(6)
    if not sb_layout:
        kb_ref[...] = kf
        vb_ref[...] = vf
        sb_refs[0][...] = qh.astype(BF16)
        return
    kb_ref[0] = kf.T
    vb_ref[0] = vf.T
    qt_ref, kp_ref, vt_ref = sb_refs
    qt_ref[0] = qh.T.astype(BF16)
    perm = _key_permutation(False)
    perm_t = _key_permutation(True)
    for blk in range(x_ref.shape[0] // SB_TK):
        rows = slice(blk * SB_TK, (blk + 1) * SB_TK)
        kp_ref[rows, :] = jnp.dot(perm, kf[rows].astype(BF16), preferred_element_type=F32).astype(BF16)
        vt_ref[0, :, rows] = jnp.dot(vf[rows].T.astype(BF16), perm_t, preferred_element_type=F32).astype(BF16)


def _inproj(x1, w_in, lb_logits, *, layer, tm, t_len, sb_layout):
    n = x1.shape[0]
    nt = t_len // tm

    def row(width):
        return pl.BlockSpec((tm, width), lambda i: (i, 0))

    cols = pl.BlockSpec((1, D_B, tm), lambda i: (i // nt, 0, i % nt))
    widths = (D_A, D_A, D_A, D_AV, D_AV, D_B, D_B, D_MODEL, D_MODEL)
    specs = [row(w) for w in widths]
    shapes = [jax.ShapeDtypeStruct((n, w), F32) for w in widths]
    if sb_layout:
        assert tm % SB_TK == 0 and t_len % tm == 0
        specs[5:7] = [cols, cols]
        shapes[5:7] = [jax.ShapeDtypeStruct((n // t_len, D_B, t_len), F32)] * 2
        specs += [cols, row(D_B), cols]
        shapes += [jax.ShapeDtypeStruct((n // t_len, D_B, t_len), BF16), jax.ShapeDtypeStruct((n, D_B), BF16),
                   jax.ShapeDtypeStruct((n // t_len, D_B, t_len), BF16)]
    else:
        specs += [row(D_B)]
        shapes += [jax.ShapeDtypeStruct((n, D_B), BF16)]
    return pl.pallas_call(
        functools.partial(_inproj_kernel, layer=layer, sb_layout=sb_layout),
        grid=(n // tm,),
        in_specs=[row(D_MODEL), _resident(w_in.shape), _resident(lb_logits.shape)],
        out_specs=specs,
        out_shape=shapes,
        compiler_params=_params(("parallel",)),
        name="inproj",
    )(x1, w_in, lb_logits)


def _cumsum_rows(x):
    n = x.shape[0]
    row = lax.broadcasted_iota(jnp.int32, x.shape, 0)
    sh = 1
    while sh < n:
        x = x + jnp.where(row >= sh, pltpu.roll(x, sh, axis=0), 0.0)
        sh *= 2
    return x


def _hgrn_kernel(q_ref, k_ref, lf_ref, v_ref, sg_ref, gn_ref, s0_ref, o_ref, sout_ref, st_scr, b_scr, *, tb):
    c = HGRN_CHUNK
    blk = q_ref.shape[1]
    ns = blk // c
    t_blk = pl.program_id(1)

    @pl.when(t_blk == 0)
    def _():
        for h in range(H_A):
            st_scr[h] = s0_ref[0, h].T

    half = V7X_SUBLANES
    row = lax.broadcasted_iota(jnp.int32, (half, K_A), 0)
    ones_kk = jnp.ones((K_A, K_A), BF16)
    nt_dims = (((1,), (1,)), ((), ()))
    quads = ns * half >= HGRN_CHUNK
    same_sub = (lax.broadcasted_iota(jnp.int32, (ns * half, ns * half), 0) // half
                == lax.broadcasted_iota(jnp.int32, (ns * half, ns * half), 1) // half)

    def block(i, carry):
        q4 = q_ref[i]
        k4 = k_ref[i]
        v4 = v_ref[i]
        sg4 = sg_ref[i]
        lf4 = lf_ref[i]
        b4 = jnp.concatenate([_cumsum_rows(lf4[j * c:(j + 1) * c]) for j in range(ns)], axis=0)
        b_scr[...] = b4
        outs = [None] * H_A

        def head(h):
            sl = slice(h * K_A, (h + 1) * K_A)
            sub = [slice(j * c, (j + 1) * c) for j in range(ns)]
            q = [q4[r, sl] for r in sub]
            k = [k4[r, sl] for r in sub]
            v = [v4[r, sl] for r in sub]
            b = [b4[r, sl] for r in sub]
            e = [x[c - 1:c, :] for x in b]

            def span(lo, hi):
                return functools.reduce(jnp.add, e[lo:hi]) if hi > lo else None

            def decayed(x, bb, extra):
                return x * jnp.exp(bb if extra is None else bb + extra)

            st = st_scr[h]
            qs = jnp.concatenate([decayed(q[n], b[n], span(0, n)) for n in range(ns)], axis=0)
            o_all = lax.dot_general(qs.astype(BF16), st.astype(BF16), nt_dims, preferred_element_type=F32)
            o = [o_all[r] for r in sub]
            kh = [k[j] * jnp.exp(e[j] - b[j]) for j in range(ns)]
            sc = []
            for j in range(ns - 1):
                qx = jnp.concatenate([decayed(q[n], b[n], span(j + 1, n)) for n in range(j + 1, ns)], axis=0)
                sc.append(lax.dot_general(qx.astype(BF16), kh[j].astype(BF16), nt_dims,
                                          preferred_element_type=F32))
            if quads:
                mid = [x[half - 1:half, :] for x in b]
                qb = jnp.concatenate([q[j][half:] * jnp.exp(b[j][half:] - mid[j]) for j in range(ns)], axis=0)
                kt = jnp.concatenate([k[j][:half] * jnp.exp(mid[j] - b[j][:half]) for j in range(ns)], axis=0)
                sq = lax.dot_general(qb.astype(BF16), kt.astype(BF16), nt_dims, preferred_element_type=F32)
            yield
            for j in range(ns - 1):
                ov = jnp.dot(sc[j].astype(BF16), v[j].astype(BF16), preferred_element_type=F32)
                for m, n in enumerate(range(j + 1, ns)):
                    o[n] = o[n] + ov[m * c:(m + 1) * c]
            if quads:
                vt = jnp.concatenate([v[j][:half] for j in range(ns)], axis=0)
                oq = jnp.dot(jnp.where(same_sub, sq, 0.0).astype(BF16), vt.astype(BF16),
                             preferred_element_type=F32)
                for j in range(ns):
                    o[j] = o[j] + jnp.concatenate([jnp.zeros((half, V_A), F32), oq[j * half:(j + 1) * half]], axis=0)
            yield
            o_h = [[x[:half], x[half:]] for x in o]
            mxu_terms = []
            for j in range(ns):
                for s in range(c):
                    ks = k_ref[i, j * c + s:j * c + s + 1, sl]
                    vs = v_ref[i, j * c + s:j * c + s + 1, sl]
                    bs = b_scr[j * c + s:j * c + s + 1, sl]
                    for p in range(2):
                        lo = p * half
                        if s >= lo + half or (quads and s < lo):
                            continue
                        w = q[j][lo:lo + half] * jnp.exp(b[j][lo:lo + half] - bs) * ks
                        if s > lo:
                            w = jnp.where(row >= s - lo, w, 0.0)
                        if (s + p) % 2:
                            mxu_terms.append((j, p, vs, w))
                        else:
                            o_h[j][p] = o_h[j][p] + jnp.sum(w, axis=-1, keepdims=True) * vs
            sums = jnp.dot(jnp.concatenate([t[3] for t in mxu_terms], axis=0).astype(BF16), ones_kk,
                           preferred_element_type=F32)
            for n, (j, p, vs, _) in enumerate(mxu_terms):
                o_h[j][p] = o_h[j][p] + sums[n * half:(n + 1) * half] * vs
            o = jnp.concatenate([x for pair in o_h for x in pair], axis=0)
            yield
            kd = jnp.concatenate([kh[j] if j == ns - 1 else kh[j] * jnp.exp(span(j + 1, ns))
                                  for j in range(ns)], axis=0)
            vv = jnp.concatenate(v, axis=0)
            upd = lax.dot_general(vv.astype(BF16), kd.astype(BF16), (((0,), (0,)), ((), ())),
                                  preferred_element_type=F32)
            st_scr[h] = st * jnp.exp(span(0, ns)) + upd
            o = o * lax.rsqrt(jnp.mean(o * o, axis=-1, keepdims=True) + RMS_EPS)
            outs[h] = o * gn_ref[:, sl] * sg4[:, sl]
            yield

        heads = [head(h) for h in range(H_A)]
        for _ in range(4):
            for g in heads:
                next(g)
        o_ref[i] = jnp.concatenate(outs, axis=1).astype(o_ref.dtype)
        return carry

    lax.fori_loop(0, tb // blk, block, 0)

    @pl.when(t_blk == pl.num_programs(1) - 1)
    def _():
        for h in range(H_A):
            sout_ref[0, h] = st_scr[h].T


def _hgrn(hq, hk, lf, hv, sg, gn, s0, *, bsz, t_len, tb):
    nt = t_len // tb
    c = min(HGRN_BLOCK, tb)
    assert tb % c == 0 and c % HGRN_CHUNK == 0
    n = bsz * t_len
    row = pl.BlockSpec((tb // c, c, D_A), lambda b, t: (b * nt + t, 0, 0))
    state = pl.BlockSpec((1, H_A, K_A, V_A), lambda b, t: (b, 0, 0, 0))
    ha, s_new = pl.pallas_call(
        functools.partial(_hgrn_kernel, tb=tb),
        grid=(bsz, nt),
        in_specs=[row, row, row, row, row, _resident(gn.shape), state],
        out_specs=[row, state],
        out_shape=[jax.ShapeDtypeStruct((n // c, c, D_AV), BF16),
                   jax.ShapeDtypeStruct((bsz, H_A, K_A, V_A), F32)],
        scratch_shapes=[pltpu.VMEM((H_A, V_A, K_A), F32), pltpu.VMEM((c, D_A), F32)],
        compiler_params=_params(("parallel", "arbitrary")),
        name="hgrn",
    )(*(a.reshape(n // c, c, D_A) for a in (hq, hk, lf, hv, sg)), gn, s0)
    return ha.reshape(n, D_AV), s_new


def _shift_up(x, sh):
    row = lax.broadcasted_iota(jnp.int32, x.shape, 0)
    return jnp.where(row < V7X_SUBLANES - sh, pltpu.roll(x, V7X_SUBLANES - sh, axis=0), 1.0)


def _sb_weights(z_ref, w_ref, carry, *, q_off=None, scale=None):
    tq = z_ref.shape[1]
    new_carry = []
    for c0 in range(0, tq, V7X_LANES):
        cols = slice(c0, c0 + V7X_LANES)
        if q_off is not None:
            sub = lax.broadcasted_iota(jnp.int32, (V7X_SUBLANES, V7X_LANES), 0) * SB_NV
            lane = lax.broadcasted_iota(jnp.int32, (V7X_SUBLANES, V7X_LANES), 1) + (q_off + c0)
        run = jnp.ones((V7X_SUBLANES, V7X_LANES), F32)
        diffs = [None] * SB_NV
        for v in reversed(range(SB_NV)):
            gv = 0.5 - 0.5 * jnp.tanh(z_ref[v * V7X_SUBLANES:(v + 1) * V7X_SUBLANES, cols])
            if q_off is not None:
                gv = jnp.where(sub + v < lane, gv, 1.0)
            nxt = run * gv
            diffs[v] = run - nxt
            run = nxt
        y = _shift_up(run, 1)
        y = y * _shift_up(y, 1)
        y = y * _shift_up(y, 2)
        y = y * _shift_up(y, 4)
        offs = carry[:, cols] * y
        if scale is not None:
            offs = offs * scale
        for v in range(0, SB_NV, 2):
            pair = jnp.concatenate([diffs[v] * offs, diffs[v + 1] * offs], axis=0)
            w_ref[v * V7X_SUBLANES:(v + 2) * V7X_SUBLANES, cols] = pair.astype(BF16)
        new_carry.append(jnp.broadcast_to((offs * run)[0:1, :], run.shape))
    return jnp.concatenate(new_carry, axis=1)


def _sb_kernel(qt_ref, k_ref, vt_ref, o_ref, z_scr, w_scr, acc_scr, *, tq, nqs, q_start):
    heads = range(SB_HEADS)

    def any_left(carries):
        return (jnp.max(functools.reduce(jnp.maximum, carries)) > 0.0).astype(jnp.int32)

    def query_block(u):
        q0 = q_start + (pl.program_id(2) * nqs + u) * tq
        n_full = q0 // SB_TK
        n_pairs = jnp.maximum(n_full, 1) // 2
        lanes = slice(u * tq, (u + 1) * tq)
        qt_all = qt_ref[0, :, lanes]
        head_of_row = lax.broadcasted_iota(jnp.int32, qt_all.shape, 0) // D_HB
        qts = [jnp.where(head_of_row == a, qt_all, jnp.zeros_like(qt_all)) for a in heads]

        def scores(j, slot):
            s0 = pl.multiple_of(jnp.maximum(j, 0) * SB_TK, SB_TK)
            kb = k_ref[0, pl.ds(s0, SB_TK), :]
            for a in heads:
                z_scr[u, slot, a] = jnp.dot(kb, qts[a], preferred_element_type=F32)

        def values(j, slot):
            s0 = pl.multiple_of(jnp.maximum(j, 0) * SB_TK, SB_TK)
            parts = [jnp.dot(vt_ref[0, a * D_HB:(a + 1) * D_HB, pl.ds(s0, SB_TK)], w_scr[u, slot, a],
                             preferred_element_type=F32) for a in heads]
            acc_scr[u] += jnp.concatenate(parts, axis=0)

        def weights(slot, carries, **kw):
            return tuple(_sb_weights(z_scr.at[u, slot, a], w_scr.at[u, slot, a], carries[a], **kw)
                         for a in heads)

        scores(n_full, 1)
        scores(n_full - 1, 0)
        acc_scr[u] = jnp.zeros(acc_scr.shape[1:], F32)
        yield
        ones = jnp.ones((V7X_SUBLANES, tq), F32)
        carries = weights(1, (ones,) * SB_HEADS, q_off=q0 - n_full * SB_TK)
        yield
        values(n_full, 1)
        scores(n_full - 2, 1)
        yield
        carries = weights(0, carries, scale=(n_full >= 1).astype(F32))
        alive = any_left(carries)
        yield

        def pair(state):
            p, carries, _ = state
            j = n_full - 2 - 2 * p
            values(j + 1, 0)
            scores(j - 1, 0)
            carries = weights(1, carries)
            alive = any_left(carries)
            scores(j - 2, 1)
            values(j, 1)
            valid = (j >= 1).astype(F32)
            return p + 1, weights(0, carries, scale=valid), alive

        def more(state):
            return jnp.logical_and(state[0] < n_pairs, state[2] > 0)

        n_done, _, _ = lax.while_loop(more, pair, (jnp.int32(0), carries, alive))
        yield
        values(n_full - 1 - 2 * n_done, 0)
        yield
        o_ref[lanes, :] = acc_scr[u].T.astype(o_ref.dtype)
        yield

    blocks = [query_block(u) for u in range(nqs)]
    for _ in range(7):
        for blk in blocks:
            next(blk)


def _sb(qt, k_perm, vt_perm, *, tq, q_start):
    bsz, _, t_q = qt.shape
    s_len = k_perm.shape[1]
    nq = t_q // tq
    assert tq <= SB_TK and q_start % SB_TK == 0 and SB_TK % tq == 0
    assert s_len % SB_TK == 0 and s_len >= q_start + nq * tq - tq + SB_TK
    group = SB_HEADS * D_HB
    nqs = 2 if nq % 2 == 0 else 1
    steps = nq // nqs
    return pl.pallas_call(
        functools.partial(_sb_kernel, tq=tq, nqs=nqs, q_start=q_start),
        grid=(bsz, H_B // SB_HEADS, steps),
        in_specs=[pl.BlockSpec((1, group, nqs * tq), lambda b, h, i: (b, h, i)),
                  pl.BlockSpec((1, s_len, group), lambda b, h, i: (b, 0, h)),
                  pl.BlockSpec((1, group, s_len), lambda b, h, i: (b, h, 0))],
        out_specs=pl.BlockSpec((nqs * tq, group), lambda b, h, i: (b * steps + i, h)),
        out_shape=jax.ShapeDtypeStruct((bsz * t_q, D_B), BF16),
        scratch_shapes=[pltpu.VMEM((nqs, 2, SB_HEADS, SB_TK, tq), F32),
                        pltpu.VMEM((nqs, 2, SB_HEADS, SB_TK, tq), BF16),
                        pltpu.VMEM((nqs, group, tq), F32)],
        compiler_params=_params(("parallel", "parallel", "arbitrary")),
        name="sb",
    )(qt, k_perm, vt_perm)


def _sb_prep_kernel(ck_ref, cv_ref, nk_ref, nv_ref, kp_ref, vt_ref, *, n_past):
    j = pl.program_id(1)
    perm = _key_permutation(False)
    perm_t = _key_permutation(True)

    @pl.when(j < n_past)
    def _():
        for blk in range(SB_PREP_ROWS // SB_TK):
            cols = slice(blk * SB_TK, (blk + 1) * SB_TK)
            kp_ref[0, cols, :] = lax.dot_general(perm, ck_ref[0, :, cols].astype(BF16), (((1,), (1,)), ((), ())),
                                                 preferred_element_type=F32).astype(BF16)
            vt_ref[0, :, cols] = jnp.dot(cv_ref[0, :, cols].astype(BF16), perm_t,
                                         preferred_element_type=F32).astype(BF16)

    @pl.when(j == n_past)
    def _():
        kp_ref[0, :SB_TK, :] = jnp.dot(perm, nk_ref[0].astype(BF16), preferred_element_type=F32).astype(BF16)
        vt_ref[0, :, :SB_TK] = jnp.dot(nv_ref[0].T.astype(BF16), perm_t, preferred_element_type=F32).astype(BF16)
        kp_ref[0, SB_TK:, :] = jnp.zeros((SB_PREP_ROWS - SB_TK, D_B), BF16)
        vt_ref[0, :, SB_TK:] = jnp.zeros((D_B, SB_PREP_ROWS - SB_TK), BF16)


def _sb_prep(cache_kt, cache_vt, new_k, new_v):
    bsz, _, p_len = cache_kt.shape
    assert p_len % SB_PREP_ROWS == 0
    n_past = p_len // SB_PREP_ROWS
    past = pl.BlockSpec((1, D_B, SB_PREP_ROWS), lambda b, j: (b, 0, jnp.minimum(j, n_past - 1)))
    new = pl.BlockSpec((1, SB_TK, D_B), lambda b, j: (b, 0, 0))
    return pl.pallas_call(
        functools.partial(_sb_prep_kernel, n_past=n_past),
        grid=(bsz, n_past + 1),
        in_specs=[past, past, new, new],
        out_specs=[pl.BlockSpec((1, SB_PREP_ROWS, D_B), lambda b, j: (b, j, 0)),
                   pl.BlockSpec((1, D_B, SB_PREP_ROWS), lambda b, j: (b, 0, j))],
        out_shape=[jax.ShapeDtypeStruct((bsz, p_len + SB_PREP_ROWS, D_B), BF16),
                   jax.ShapeDtypeStruct((bsz, D_B, p_len + SB_PREP_ROWS), BF16)],
        compiler_params=_params(("parallel", "arbitrary")),
        name="sb_prep",
    )(cache_kt, cache_vt, new_k, new_v)


def _mix_ffn_kernel(x_ref, ha_ref, hb_ref, ga_ref, gb_ref, wa_ref, wb_ref, wo_ref, g2_ref, b2_ref,
                    wg_ref, wu_ref, wd_ref, g3_ref, b3_ref, o_ref, *, alpha):
    pa = jnp.dot(ha_ref[...], wa_ref[...], preferred_element_type=F32)
    pb = jnp.dot(hb_ref[...], wb_ref[...], preferred_element_type=F32)
    merged = ga_ref[...] * pa + gb_ref[...] * pb
    mix = jnp.dot(merged.astype(BF16), wo_ref[...], preferred_element_type=F32)
    x2 = _layer_norm(alpha * x_ref[...] + mix, g2_ref[...], b2_ref[...])
    ff = _swiglu(x2.astype(BF16), wg_ref, wu_ref, wd_ref)
    o_ref[...] = _layer_norm(alpha * x2 + 0.5 * ff, g3_ref[...], b3_ref[...])


def _mix_ffn(x1, ha, hb, ga, gb, wa, wb, wo, g2, b2, wg, wu, wd, g3, b3, *, alpha, tm):
    n = x1.shape[0]

    def row(width):
        return pl.BlockSpec((tm, width), lambda i: (i, 0))

    consts = (wa, wb, wo, g2, b2, wg, wu, wd, g3, b3)
    return pl.pallas_call(
        functools.partial(_mix_ffn_kernel, alpha=alpha),
        grid=(n // tm,),
        in_specs=[row(D_MODEL), row(D_AV), row(D_B), row(D_MODEL), row(D_MODEL)]
                 + [_resident(c.shape) for c in consts],
        out_specs=row(D_MODEL),
        out_shape=jax.ShapeDtypeStruct((n, D_MODEL), F32),
        compiler_params=_params(("parallel",)),
        name="mix_ffn",
    )(x1, ha, hb, ga, gb, *consts)


def _channels_first(a):
    bsz, p_len = a.shape[:2]
    return jnp.transpose(a, (0, 2, 3, 1)).reshape(bsz, D_B, p_len)


def _pick_tile(n, cap):
    t = min(n, cap)
    assert n % t == 0
    return t


def _trunk(x, s0, past_k, past_v, w):
    bsz, t_len, _ = x.shape
    n = bsz * t_len
    depth = w["w_in"].shape[0]
    alpha = (2 * depth) ** 0.25
    tm = _pick_tile(n, 512)
    xf = x.reshape(n, D_MODEL)
    ks, vs, ss = [], [], []
    for l in range(depth):
        x1 = _ffn_ln(xf, w["ffn1_wg"][l], w["ffn1_wu"][l], w["ffn1_wd"][l], w["ln1_g"][l], w["ln1_b"][l],
                     alpha=alpha, tm=_pick_tile(n, 1024))
        sb_layout = past_k is None
        hq, hk, lf, hv, sg, kb, vb, ga, gb, *sb_ops = _inproj(
            x1, w["w_in"][l], w["lb_logits"], layer=l, tm=tm, t_len=t_len, sb_layout=sb_layout)
        ha, s_new = _hgrn(hq, hk, lf, hv, sg, w["hgrn_norm_g"][l], s0[l],
                          bsz=bsz, t_len=t_len, tb=_pick_tile(t_len, 1024))

        if sb_layout:
            qt, k_perm, vt_perm = sb_ops
            hb = _sb(qt, k_perm.reshape(bsz, t_len, D_B), vt_perm, tq=SB_TK, q_start=0)
        else:
            q_start = past_k.shape[2]
            tq = V7X_LANES
            assert t_len <= tq
            pad_new = ((0, 0), (0, SB_TK - t_len), (0, 0))
            k_perm, vt_perm = _sb_prep(_channels_first(past_k[l]), _channels_first(past_v[l]),
                                       jnp.pad(kb.reshape(bsz, t_len, D_B), pad_new),
                                       jnp.pad(vb.reshape(bsz, t_len, D_B), pad_new))
            q3 = jnp.pad(sb_ops[0].reshape(bsz, t_len, D_B), ((0, 0), (0, tq - t_len), (0, 0)))
            hb = _sb(jnp.swapaxes(q3, 1, 2), k_perm, vt_perm, tq=tq, q_start=q_start)
            hb = hb.reshape(bsz, tq, D_B)[:, :t_len].reshape(n, D_B)

        xf = _mix_ffn(x1, ha, hb, ga, gb, w["w_branch_a"][l], w["w_branch_b"][l], w["w_out"][l],
                      w["ln2_g"][l], w["ln2_b"][l], w["ffn2_wg"][l], w["ffn2_wu"][l], w["ffn2_wd"][l],
                      w["ln3_g"][l], w["ln3_b"][l], alpha=alpha, tm=tm)
        if sb_layout:
            kb, vb = (jnp.transpose(a.reshape(bsz, H_B, D_HB, t_len), (0, 3, 1, 2)) for a in (kb, vb))
        ks.append(kb.reshape(bsz, t_len, H_B, D_HB))
        vs.append(vb.reshape(bsz, t_len, H_B, D_HB))
        ss.append(s_new)
    return xf.reshape(bsz, t_len, D_MODEL), jnp.stack(ks), jnp.stack(vs), jnp.stack(ss)


def kernel(x_prompt, x_sample, cache_sb_k, cache_sb_v, state_hgrn, ffn1_wg, ffn1_wu, ffn1_wd, ln1_g, ln1_b,
           w_in, lb_logits, hgrn_norm_g, w_branch_a, w_branch_b, w_out, ln2_g, ln2_b,
           ffn2_wg, ffn2_wu, ffn2_wd, ln3_g, ln3_b):
    depth = w_in.shape[0]

    def vec(p):
        return p[:, None, :]

    w = dict(
        ffn1_wg=ffn1_wg.astype(BF16), ffn1_wu=ffn1_wu.astype(BF16), ffn1_wd=ffn1_wd.astype(BF16),
        ln1_g=vec(ln1_g), ln1_b=vec(ln1_b), w_in=w_in.astype(BF16), lb_logits=lb_logits,
        hgrn_norm_g=vec(hgrn_norm_g), w_branch_a=w_branch_a.astype(BF16), w_branch_b=w_branch_b.astype(BF16),
        w_out=w_out.astype(BF16), ln2_g=vec(ln2_g), ln2_b=vec(ln2_b),
        ffn2_wg=ffn2_wg.astype(BF16), ffn2_wu=ffn2_wu.astype(BF16), ffn2_wd=ffn2_wd.astype(BF16),
        ln3_g=vec(ln3_g), ln3_b=vec(ln3_b))
    s0_prompt = jnp.zeros((depth, x_prompt.shape[0], H_A, K_A, V_A), F32)
    y_p, k_p, v_p, s_p = _trunk(x_prompt, s0_prompt, None, None, w)
    y_s, k_s, v_s, s_s = _trunk(x_sample, state_hgrn, cache_sb_k, cache_sb_v, w)
    return (y_p, y_s, k_p, v_p, s_p, k_s, v_s, s_s)
```

```python
import functools

import jax
import jax.numpy as jnp
from jax import lax
from jax.experimental import pallas as pl
from jax.experimental.pallas import tpu as pltpu

F32 = jnp.float32
BF16 = jnp.bfloat16

D_MODEL = 1024
H_A, K_A, V_A = 4, 128, 128
D_A = H_A * K_A
D_AV = H_A * V_A
H_B, D_HB = 8, 64
D_B = H_B * D_HB
D_FF = 2816
N_IN = 2 * D_A + 2 * D_AV + 3 * D_B + 2 * D_MODEL
LN_EPS = 1e-5
RMS_EPS = 1e-6
SB_SCALE = D_HB ** -0.5

V7X_LANES = 128
V7X_SUBLANES = 8
V7X_MXU_DIM = 256
V7X_VMEM_BYTES = 64 * 1024 * 1024
V7X_VMEM_LIMIT_BYTES = V7X_VMEM_BYTES * 7 // 8

FF_CHUNK = V7X_MXU_DIM
HGRN_CHUNK = 2 * V7X_SUBLANES
HGRN_BLOCK = 4 * HGRN_CHUNK
SB_TK = V7X_MXU_DIM
SB_NV = SB_TK // V7X_SUBLANES
SB_HEADS = V7X_MXU_DIM // D_HB
SB_PREP_ROWS = 4 * SB_TK


def _params(sem):
    return pltpu.CompilerParams(dimension_semantics=sem, vmem_limit_bytes=V7X_VMEM_LIMIT_BYTES)


def _resident(shape):
    nd = len(shape)
    return pl.BlockSpec(shape, lambda *_: (0,) * nd, pipeline_mode=pl.Buffered(1))


def _layer_norm(y, g, b):
    mu = jnp.mean(y, axis=-1, keepdims=True)
    d = y - mu
    var = jnp.mean(d * d, axis=-1, keepdims=True)
    return d * lax.rsqrt(var + LN_EPS) * g + b


def _silu(x):
    return x * jax.nn.sigmoid(x)


def _swiglu(xb, wg_ref, wu_ref, wd_ref):
    acc = jnp.zeros((xb.shape[0], D_MODEL), F32)
    for c in range(D_FF // FF_CHUNK):
        sl = slice(c * FF_CHUNK, (c + 1) * FF_CHUNK)
        g = jnp.dot(xb, wg_ref[:, sl], preferred_element_type=F32)
        u = jnp.dot(xb, wu_ref[:, sl], preferred_element_type=F32)
        h = (_silu(g) * u).astype(BF16)
        acc = acc + jnp.dot(h, wd_ref[sl, :], preferred_element_type=F32)
    return acc


def _ffn_ln_kernel(x_ref, wg_ref, wu_ref, wd_ref, g_ref, b_ref, o_ref, *, alpha):
    x = x_ref[...]
    ff = _swiglu(x.astype(BF16), wg_ref, wu_ref, wd_ref)
    o_ref[...] = _layer_norm(alpha * x + 0.5 * ff, g_ref[...], b_ref[...])


def _ffn_ln(x, wg, wu, wd, g, b, *, alpha, tm):
    n = x.shape[0]
    row = pl.BlockSpec((tm, D_MODEL), lambda i: (i, 0))
    return pl.pallas_call(
        functools.partial(_ffn_ln_kernel, alpha=alpha),
        grid=(n // tm,),
        in_specs=[row, _resident(wg.shape), _resident(wu.shape), _resident(wd.shape),
                  _resident(g.shape), _resident(b.shape)],
        out_specs=row,
        out_shape=jax.ShapeDtypeStruct((n, D_MODEL), F32),
        compiler_params=_params(("parallel",)),
        name="ffn_ln",
    )(x, wg, wu, wd, g, b)


_CUTS = (0, D_A, 2 * D_A, 2 * D_A + D_AV, 2 * D_A + 2 * D_AV,
         2 * D_A + 2 * D_AV + D_B, 2 * D_A + 2 * D_AV + 2 * D_B,
         2 * D_A + 2 * D_AV + 3 * D_B, 2 * D_A + 2 * D_AV + 3 * D_B + D_MODEL, N_IN)


def _key_permutation(transposed):
    i = lax.broadcasted_iota(jnp.int32, (SB_TK, SB_TK), 1 if transposed else 0)
    j = lax.broadcasted_iota(jnp.int32, (SB_TK, SB_TK), 0 if transposed else 1)
    src = (i % V7X_SUBLANES) * SB_NV + i // V7X_SUBLANES
    return jnp.where(j == src, 1.0, 0.0).astype(BF16)


def _inproj_kernel(x_ref, w_ref, lbl_ref, hq_ref, hk_ref, lf_ref, hv_ref, sg_ref,
                   kb_ref, vb_ref, ga_ref, gb_ref, *sb_refs, layer, sb_layout):
    xb = x_ref[...].astype(BF16)

    def proj(i):
        return jnp.dot(xb, w_ref[:, _CUTS[i]:_CUTS[i + 1]], preferred_element_type=F32)

    hq_ref[...] = _silu(proj(0))
    rows = [lbl_ref[i:i + 1, :] for i in range(lbl_ref.shape[0])]
    m = functools.reduce(jnp.maximum, rows)
    ex = [jnp.exp(r - m) for r in rows]
    one_minus_lb = sum(ex[layer + 1:]) / sum(ex)
    hk = one_minus_lb * jax.nn.sigmoid(-proj(1))
    hk_ref[...] = hk
    lf_ref[...] = jnp.log1p(-hk)
    hv_ref[...] = proj(2)
    sg_ref[...] = _silu(proj(3))
    ga_ref[...] = jax.nn.sigmoid(proj(7))
    gb_ref[...] = jax.nn.sigmoid(proj(8))
    qh = proj(4) * (0.5 * SB_SCALE)
    kf = proj(5)
    vf = proj(6)
    if not sb_layout:
        kb_ref[...] = kf
        vb_ref[...] = vf
        sb_refs[0][...] = qh.astype(BF16)
        return
    kb_ref[0] = kf.T
    vb_ref[0] = vf.T
    qt_ref, kp_ref, vt_ref = sb_refs
    qt_ref[0] = qh.T.astype(BF16)
    perm = _key_permutation(False)
    perm_t = _key_permutation(True)
    for blk in range(x_ref.shape[0] // SB_TK):
        rows = slice(blk * SB_TK, (blk + 1) * SB_TK)
        kp_ref[rows, :] = jnp.dot(perm, kf[rows].astype(BF16), preferred_element_type=F32).astype(BF16)
        vt_ref[0, :, rows] = jnp.dot(vf[rows].T.astype(BF16), perm_t, preferred_element_type=F32).astype(BF16)


def _inproj(x1, w_in, lb_logits, *, layer, tm, t_len, sb_layout):
    n = x1.shape[0]
    nt = t_len // tm

    def row(width):
        return pl.BlockSpec((tm, width), lambda i: (i, 0))

    cols = pl.BlockSpec((1, D_B, tm), lambda i: (i // nt, 0, i % nt))
    widths = (D_A, D_A, D_A, D_AV, D_AV, D_B, D_B, D_MODEL, D_MODEL)
    specs = [row(w) for w in widths]
    shapes = [jax.ShapeDtypeStruct((n, w), F32) for w in widths]
    if sb_layout:
        assert tm % SB_TK == 0 and t_len % tm == 0
        specs[5:7] = [cols, cols]
        shapes[5:7] = [jax.ShapeDtypeStruct((n // t_len, D_B, t_len), F32)] * 2
        specs += [cols, row(D_B), cols]
        shapes += [jax.ShapeDtypeStruct((n // t_len, D_B, t_len), BF16), jax.ShapeDtypeStruct((n, D_B), BF16),
                   jax.ShapeDtypeStruct((n // t_len, D_B, t_len), BF16)]
    else:
        specs += [row(D_B)]
        shapes += [jax.ShapeDtypeStruct((n, D_B), BF16)]
    return pl.pallas_call(
        functools.partial(_inproj_kernel, layer=layer, sb_layout=sb_layout),
        grid=(n // tm,),
        in_specs=[row(D_MODEL), _resident(w_in.shape), _resident(lb_logits.shape)],
        out_specs=specs,
        out_shape=shapes,
        compiler_params=_params(("parallel",)),
        name="inproj",
    )(x1, w_in, lb_logits)


def _cumsum_rows(x):
    n = x.shape[0]
    row = lax.broadcasted_iota(jnp.int32, x.shape, 0)
    sh = 1
    while sh < n:
        x = x + jnp.where(row >= sh, pltpu.roll(x, sh, axis=0), 0.0)
        sh *= 2
    return x


def _hgrn_kernel(q_ref, k_ref, lf_ref, v_ref, sg_ref, gn_ref, s0_ref, o_ref, sout_ref, st_scr, b_scr, *, tb):
    c = HGRN_CHUNK
    blk = q_ref.shape[1]
    ns = blk // c
    t_blk = pl.program_id(1)

    @pl.when(t_blk == 0)
    def _():
        for h in range(H_A):
            st_scr[h] = s0_ref[0, h].T

    half = V7X_SUBLANES
    row = lax.broadcasted_iota(jnp.int32, (half, K_A), 0)
    ones_kk = jnp.ones((K_A, K_A), BF16)
    nt_dims = (((1,), (1,)), ((), ()))
    quads = ns * half >= HGRN_CHUNK
    same_sub = (lax.broadcasted_iota(jnp.int32, (ns * half, ns * half), 0) // half
                == lax.broadcasted_iota(jnp.int32, (ns * half, ns * half), 1) // half)

    def block(i, carry):
        q4 = q_ref[i]
        k4 = k_ref[i]
        v4 = v_ref[i]
        sg4 = sg_ref[i]
        lf4 = lf_ref[i]
        b4 = jnp.concatenate([_cumsum_rows(lf4[j * c:(j + 1) * c]) for j in range(ns)], axis=0)
        b_scr[...] = b4
        outs = [None] * H_A

        def head(h):
            sl = slice(h * K_A, (h + 1) * K_A)
            sub = [slice(j * c, (j + 1) * c) for j in range(ns)]
            q = [q4[r, sl] for r in sub]
            k = [k4[r, sl] for r in sub]
            v = [v4[r, sl] for r in sub]
            b = [b4[r, sl] for r in sub]
            e = [x[c - 1:c, :] for x in b]

            def span(lo, hi):
                return functools.reduce(jnp.add, e[lo:hi]) if hi > lo else None

            def decayed(x, bb, extra):
                return x * jnp.exp(bb if extra is None else bb + extra)

            st = st_scr[h]
            qs = jnp.concatenate([decayed(q[n], b[n], span(0, n)) for n in range(ns)], axis=0)
            o_all = lax.dot_general(qs.astype(BF16), st.astype(BF16), nt_dims, preferred_element_type=F32)
            o = [o_all[r] for r in sub]
            kh = [k[j] * jnp.exp(e[j] - b[j]) for j in range(ns)]
            sc = []
            for j in range(ns - 1):
                qx = jnp.concatenate([decayed(q[n], b[n], span(j + 1, n)) for n in range(j + 1, ns)], axis=0)
                sc.append(lax.dot_general(qx.astype(BF16), kh[j].astype(BF16), nt_dims,
                                          preferred_element_type=F32))
            if quads:
                mid = [x[half - 1:half, :] for x in b]
                qb = jnp.concatenate([q[j][half:] * jnp.exp(b[j][half:] - mid[j]) for j in range(ns)], axis=0)
                kt = jnp.concatenate([k[j][:half] * jnp.exp(mid[j] - b[j][:half]) for j in range(ns)], axis=0)
                sq = lax.dot_general(qb.astype(BF16), kt.astype(BF16), nt_dims, preferred_element_type=F32)
            yield
            for j in range(ns - 1):
                ov = jnp.dot(sc[j].astype(BF16), v[j].astype(BF16), preferred_element_type=F32)
                for m, n in enumerate(range(j + 1, ns)):
                    o[n] = o[n] + ov[m * c:(m + 1) * c]
            if quads:
                vt = jnp.concatenate([v[j][:half] for j in range(ns)], axis=0)
                oq = jnp.dot(jnp.where(same_sub, sq, 0.0).astype(BF16), vt.astype(BF16),
                             preferred_element_type=F32)
                for j in range(ns):
                    o[j] = o[j] + jnp.concatenate([jnp.zeros((half, V_A), F32), oq[j * half:(j + 1) * half]], axis=0)
            yield
            o_h = [[x[:half], x[half:]] for x in o]
            mxu_terms = []
            for j in range(ns):
                for s in range(c):
                    ks = k_ref[i, j * c + s:j * c + s + 1, sl]
                    vs = v_ref[i, j * c + s:j * c + s + 1, sl]
                    bs = b_scr[j * c + s:j * c + s + 1, sl]
                    for p in range(2):
                        lo = p * half
                        if s >= lo + half or (quads and s < lo):
                            continue
                        w = q[j][lo:lo + half] * jnp.exp(b[j][lo:lo + half] - bs) * ks
                        if s > lo:
                            w = jnp.where(row >= s - lo, w, 0.0)
                        if (s + p) % 2:
                            mxu_terms.append((j, p, vs, w))
                        else:
                            o_h[j][p] = o_h[j][p] + jnp.sum(w, axis=-1, keepdims=True) * vs
            sums = jnp.dot(jnp.concatenate([t[3] for t in mxu_terms], axis=0).astype(BF16), ones_kk,
                           preferred_element_type=F32)
            for n, (j, p, vs, _) in enumerate(mxu_terms):
                o_h[j][p] = o_h[j][p] + sums[n * half:(n + 1) * half] * vs
            o = jnp.concatenate([x for pair in o_h for x in pair], axis=0)
            yield
            kd = jnp.concatenate([kh[j] if j == ns - 1 else kh[j] * jnp.exp(span(j + 1, ns))
                                  for j in range(ns)], axis=0)
            vv = jnp.concatenate(v, axis=0)
            upd = lax.dot_general(vv.astype(BF16), kd.astype(BF16), (((0,), (0,)), ((), ())),
                                  preferred_element_type=F32)
            st_scr[h] = st * jnp.exp(span(0, ns)) + upd
            o = o * lax.rsqrt(jnp.mean(o * o, axis=-1, keepdims=True) + RMS_EPS)
            outs[h] = o * gn_ref[:, sl] * sg4[:, sl]
            yield

        heads = [head(h) for h in range(H_A)]
        for _ in range(4):
            for g in heads:
                next(g)
        o_ref[i] = jnp.concatenate(outs, axis=1).astype(o_ref.dtype)
        return carry

    lax.fori_loop(0, tb // blk, block, 0)

    @pl.when(t_blk == pl.num_programs(1) - 1)
    def _():
        for h in range(H_A):
            sout_ref[0, h] = st_scr[h].T


def _hgrn(hq, hk, lf, hv, sg, gn, s0, *, bsz, t_len, tb):
    nt = t_len // tb
    c = min(HGRN_BLOCK, tb)
    assert tb % c == 0 and c % HGRN_CHUNK == 0
    n = bsz * t_len
    row = pl.BlockSpec((tb // c, c, D_A), lambda b, t: (b * nt + t, 0, 0))
    state = pl.BlockSpec((1, H_A, K_A, V_A), lambda b, t: (b, 0, 0, 0))
    ha, s_new = pl.pallas_call(
        functools.partial(_hgrn_kernel, tb=tb),
        grid=(bsz, nt),
        in_specs=[row, row, row, row, row, _resident(gn.shape), state],
        out_specs=[row, state],
        out_shape=[jax.ShapeDtypeStruct((n // c, c, D_AV), BF16),
                   jax.ShapeDtypeStruct((bsz, H_A, K_A, V_A), F32)],
        scratch_shapes=[pltpu.VMEM((H_A, V_A, K_A), F32), pltpu.VMEM((c, D_A), F32)],
        compiler_params=_params(("parallel", "arbitrary")),
        name="hgrn",
    )(*(a.reshape(n // c, c, D_A) for a in (hq, hk, lf, hv, sg)), gn, s0)
    return ha.reshape(n, D_AV), s_new


def _shift_up(x, sh):
    row = lax.broadcasted_iota(jnp.int32, x.shape, 0)
    return jnp.where(row < V7X_SUBLANES - sh, pltpu.roll(x, V7X_SUBLANES - sh, axis=0), 1.0)


def _sb_weights(z_ref, w_ref, carry, *, q_off=None, scale=None):
    tq = z_ref.shape[1]
    new_carry = []
    for c0 in range(0, tq, V7X_LANES):
        cols = slice(c0, c0 + V7X_LANES)
        if q_off is not None:
            sub = lax.broadcasted_iota(jnp.int32, (V7X_SUBLANES, V7X_LANES), 0) * SB_NV
            lane = lax.broadcasted_iota(jnp.int32, (V7X_SUBLANES, V7X_LANES), 1) + (q_off + c0)
        run = jnp.ones((V7X_SUBLANES, V7X_LANES), F32)
        diffs = [None] * SB_NV
        for v in reversed(range(SB_NV)):
            gv = 0.5 - 0.5 * jnp.tanh(z_ref[v * V7X_SUBLANES:(v + 1) * V7X_SUBLANES, cols])
            if q_off is not None:
                gv = jnp.where(sub + v < lane, gv, 1.0)
            nxt = run * gv
            diffs[v] = run - nxt
            run = nxt
        y = _shift_up(run, 1)
        y = y * _shift_up(y, 1)
        y = y * _shift_up(y, 2)
        y = y * _shift_up(y, 4)
        offs = carry[:, cols] * y
        if scale is not None:
            offs = offs * scale
        for v in range(0, SB_NV, 2):
            pair = jnp.concatenate([diffs[v] * offs, diffs[v + 1] * offs], axis=0)
            w_ref[v * V7X_SUBLANES:(v + 2) * V7X_SUBLANES, cols] = pair.astype(BF16)
        new_carry.append(jnp.broadcast_to((offs * run)[0:1, :], run.shape))
    return jnp.concatenate(new_carry, axis=1)


def _sb_kernel(qt_ref, k_ref, vt_ref, o_ref, z_scr, w_scr, acc_scr, *, tq, nqs, q_start):
    heads = range(SB_HEADS)

    def any_left(carries):
        return (jnp.max(functools.reduce(jnp.maximum, carries)) > 0.0).astype(jnp.int32)

    def query_block(u):
        q0 = q_start + (pl.program_id(2) * nqs + u) * tq
        n_full = q0 // SB_TK
        n_pairs = jnp.maximum(n_full, 1) // 2
        lanes = slice(u * tq, (u + 1) * tq)
        qt_all = qt_ref[0, :, lanes]
        head_of_row = lax.broadcasted_iota(jnp.int32, qt_all.shape, 0) // D_HB
        qts = [jnp.where(head_of_row == a, qt_all, jnp.zeros_like(qt_all)) for a in heads]

        def scores(j, slot):
            s0 = pl.multiple_of(jnp.maximum(j, 0) * SB_TK, SB_TK)
            kb = k_ref[0, pl.ds(s0, SB_TK), :]
            for a in heads:
                z_scr[u, slot, a] = jnp.dot(kb, qts[a], preferred_element_type=F32)

        def values(j, slot):
            s0 = pl.multiple_of(jnp.maximum(j, 0) * SB_TK, SB_TK)
            parts = [jnp.dot(vt_ref[0, a * D_HB:(a + 1) * D_HB, pl.ds(s0, SB_TK)], w_scr[u, slot, a],
                             preferred_element_type=F32) for a in heads]
            acc_scr[u] += jnp.concatenate(parts, axis=0)

        def weights(slot, carries, **kw):
            return tuple(_sb_weights(z_scr.at[u, slot, a], w_scr.at[u, slot, a], carries[a], **kw)
                         for a in heads)

        scores(n_full, 1)
        scores(n_full - 1, 0)
        acc_scr[u] = jnp.zeros(acc_scr.shape[1:], F32)
        yield
        ones = jnp.ones((V7X_SUBLANES, tq), F32)
        carries = weights(1, (ones,) * SB_HEADS, q_off=q0 - n_full * SB_TK)
        yield
        values(n_full, 1)
        scores(n_full - 2, 1)
        yield
        carries = weights(0, carries, scale=(n_full >= 1).astype(F32))
        alive = any_left(carries)
        yield

        def pair(state):
            p, carries, _ = state
            j = n_full - 2 - 2 * p
            values(j + 1, 0)
            scores(j - 1, 0)
            carries = weights(1, carries)
            alive = any_left(carries)
            scores(j - 2, 1)
            values(j, 1)
            valid = (j >= 1).astype(F32)
            return p + 1, weights(0, carries, scale=valid), alive

        def more(state):
            return jnp.logical_and(state[0] < n_pairs, state[2] > 0)

        n_done, _, _ = lax.while_loop(more, pair, (jnp.int32(0), carries, alive))
        yield
        values(n_full - 1 - 2 * n_done, 0)
        yield
        o_ref[lanes, :] = acc_scr[u].T.astype(o_ref.dtype)
        yield

    blocks = [query_block(u) for u in range(nqs)]
    for _ in range(7):
        for blk in blocks:
            next(blk)


def _sb(qt, k_perm, vt_perm, *, tq, q_start):
    bsz, _, t_q = qt.shape
    s_len = k_perm.shape[1]
    nq = t_q // tq
    assert tq <= SB_TK and q_start % SB_TK == 0 and SB_TK % tq == 0
    assert s_len % SB_TK == 0 and s_len >= q_start + nq * tq - tq + SB_TK
    group = SB_HEADS * D_HB
    nqs = 4 if nq % 4 == 0 else 2 if nq % 2 == 0 else 1
    steps = nq // nqs
    return pl.pallas_call(
        functools.partial(_sb_kernel, tq=tq, nqs=nqs, q_start=q_start),
        grid=(bsz, H_B // SB_HEADS, steps),
        in_specs=[pl.BlockSpec((1, group, nqs * tq), lambda b, h, i: (b, h, i)),
                  pl.BlockSpec((1, s_len, group), lambda b, h, i: (b, 0, h)),
                  pl.BlockSpec((1, group, s_len), lambda b, h, i: (b, h, 0))],
        out_specs=pl.BlockSpec((nqs * tq, group), lambda b, h, i: (b * steps + i, h)),
        out_shape=jax.ShapeDtypeStruct((bsz * t_q, D_B), BF16),
        scratch_shapes=[pltpu.VMEM((nqs, 2, SB_HEADS, SB_TK, tq), F32),
                        pltpu.VMEM((nqs, 2, SB_HEADS, SB_TK, tq), BF16),
                        pltpu.VMEM((nqs, group, tq), F32)],
        compiler_params=_params(("parallel", "parallel", "arbitrary")),
        name="sb",
    )(qt, k_perm, vt_perm)


def _sb_prep_kernel(ck_ref, cv_ref, nk_ref, nv_ref, kp_ref, vt_ref, *, n_past):
    j = pl.program_id(1)
    perm = _key_permutation(False)
    perm_t = _key_permutation(True)

    @pl.when(j < n_past)
    def _():
        for blk in range(SB_PREP_ROWS // SB_TK):
            cols = slice(blk * SB_TK, (blk + 1) * SB_TK)
            kp_ref[0, cols, :] = lax.dot_general(perm, ck_ref[0, :, cols].astype(BF16), (((1,), (1,)), ((), ())),
                                                 preferred_element_type=F32).astype(BF16)
            vt_ref[0, :, cols] = jnp.dot(cv_ref[0, :, cols].astype(BF16), perm_t,
                                         preferred_element_type=F32).astype(BF16)

    @pl.when(j == n_past)
    def _():
        kp_ref[0, :SB_TK, :] = jnp.dot(perm, nk_ref[0].astype(BF16), preferred_element_type=F32).astype(BF16)
        vt_ref[0, :, :SB_TK] = jnp.dot(nv_ref[0].T.astype(BF16), perm_t, preferred_element_type=F32).astype(BF16)
        kp_ref[0, SB_TK:, :] = jnp.zeros((SB_PREP_ROWS - SB_TK, D_B), BF16)
        vt_ref[0, :, SB_TK:] = jnp.zeros((D_B, SB_PREP_ROWS - SB_TK), BF16)


def _sb_prep(cache_kt, cache_vt, new_k, new_v):
    bsz, _, p_len = cache_kt.shape
    assert p_len % SB_PREP_ROWS == 0
    n_past = p_len // SB_PREP_ROWS
    past = pl.BlockSpec((1, D_B, SB_PREP_ROWS), lambda b, j: (b, 0, jnp.minimum(j, n_past - 1)))
    new = pl.BlockSpec((1, SB_TK, D_B), lambda b, j: (b, 0, 0))
    return pl.pallas_call(
        functools.partial(_sb_prep_kernel, n_past=n_past),
        grid=(bsz, n_past + 1),
        in_specs=[past, past, new, new],
        out_specs=[pl.BlockSpec((1, SB_PREP_ROWS, D_B), lambda b, j: (b, j, 0)),
                   pl.BlockSpec((1, D_B, SB_PREP_ROWS), lambda b, j: (b, 0, j))],
        out_shape=[jax.ShapeDtypeStruct((bsz, p_len + SB_PREP_ROWS, D_B), BF16),
                   jax.ShapeDtypeStruct((bsz, D_B, p_len + SB_PREP_ROWS), BF16)],
        compiler_params=_params(("parallel", "arbitrary")),
        name="sb_prep",
    )(cache_kt, cache_vt, new_k, new_v)


def _mix_ffn_kernel(x_ref, ha_ref, hb_ref, ga_ref, gb_ref, wa_ref, wb_ref, wo_ref, g2_ref, b2_ref,
                    wg_ref, wu_ref, wd_ref, g3_ref, b3_ref, o_ref, *, alpha):
    pa = jnp.dot(ha_ref[...], wa_ref[...], preferred_element_type=F32)
    pb = jnp.dot(hb_ref[...], wb_ref[...], preferred_element_type=F32)
    merged = ga_ref[...] * pa + gb_ref[...] * pb
    mix = jnp.dot(merged.astype(BF16), wo_ref[...], preferred_element_type=F32)
    x2 = _layer_norm(alpha * x_ref[...] + mix, g2_ref[...], b2_ref[...])
    ff = _swiglu(x2.astype(BF16), wg_ref, wu_ref, wd_ref)
    o_ref[...] = _layer_norm(alpha * x2 + 0.5 * ff, g3_ref[...], b3_ref[...])


def _mix_ffn(x1, ha, hb, ga, gb, wa, wb, wo, g2, b2, wg, wu, wd, g3, b3, *, alpha, tm):
    n = x1.shape[0]

    def row(width):
        return pl.BlockSpec((tm, width), lambda i: (i, 0))

    consts = (wa, wb, wo, g2, b2, wg, wu, wd, g3, b3)
    return pl.pallas_call(
        functools.partial(_mix_ffn_kernel, alpha=alpha),
        grid=(n // tm,),
        in_specs=[row(D_MODEL), row(D_AV), row(D_B), row(D_MODEL), row(D_MODEL)]
                 + [_resident(c.shape) for c in consts],
        out_specs=row(D_MODEL),
        out_shape=jax.ShapeDtypeStruct((n, D_MODEL), F32),
        compiler_params=_params(("parallel",)),
        name="mix_ffn",
    )(x1, ha, hb, ga, gb, *consts)


def _channels_first(a):
    bsz, p_len = a.shape[:2]
    return jnp.transpose(a, (0, 2, 3, 1)).reshape(bsz, D_B, p_len)


def _pick_tile(n, cap):
    t = min(n, cap)
    assert n % t == 0
    return t


def _trunk(x, s0, past_k, past_v, w):
    bsz, t_len, _ = x.shape
    n = bsz * t_len
    depth = w["w_in"].shape[0]
    alpha = (2 * depth) ** 0.25
    tm = _pick_tile(n, 512)
    xf = x.reshape(n, D_MODEL)
    ks, vs, ss = [], [], []
    for l in range(depth):
        x1 = _ffn_ln(xf, w["ffn1_wg"][l], w["ffn1_wu"][l], w["ffn1_wd"][l], w["ln1_g"][l], w["ln1_b"][l],
                     alpha=alpha, tm=_pick_tile(n, 1024))
        sb_layout = past_k is None
        hq, hk, lf, hv, sg, kb, vb, ga, gb, *sb_ops = _inproj(
            x1, w["w_in"][l], w["lb_logits"], layer=l, tm=tm, t_len=t_len, sb_layout=sb_layout)
        ha, s_new = _hgrn(hq, hk, lf, hv, sg, w["hgrn_norm_g"][l], s0[l],
                          bsz=bsz, t_len=t_len, tb=_pick_tile(t_len, 1024))

        if sb_layout:
            qt, k_perm, vt_perm = sb_ops
            hb = _sb(qt, k_perm.reshape(bsz, t_len, D_B), vt_perm, tq=SB_TK, q_start=0)
        else:
            q_start = past_k.shape[2]
            tq = V7X_LANES
            assert t_len <= tq
            pad_new = ((0, 0), (0, SB_TK - t_len), (0, 0))
            k_perm, vt_perm = _sb_prep(_channels_first(past_k[l]), _channels_first(past_v[l]),
                                       jnp.pad(kb.reshape(bsz, t_len, D_B), pad_new),
                                       jnp.pad(vb.reshape(bsz, t_len, D_B), pad_new))
            q3 = jnp.pad(sb_ops[0].reshape(bsz, t_len, D_B), ((0, 0), (0, tq - t_len), (0, 0)))
            hb = _sb(jnp.swapaxes(q3, 1, 2), k_perm, vt_perm, tq=tq, q_start=q_start)
            hb = hb.reshape(bsz, tq, D_B)[:, :t_len].reshape(n, D_B)

        xf = _mix_ffn(x1, ha, hb, ga, gb, w["w_branch_a"][l], w["w_branch_b"][l], w["w_out"][l],
                      w["ln2_g"][l], w["ln2_b"][l], w["ffn2_wg"][l], w["ffn2_wu"][l], w["ffn2_wd"][l],
                      w["ln3_g"][l], w["ln3_b"][l], alpha=alpha, tm=tm)
        if sb_layout:
            kb, vb = (jnp.transpose(a.reshape(bsz, H_B, D_HB, t_len), (0, 3, 1, 2)) for a in (kb, vb))
        ks.append(kb.reshape(bsz, t_len, H_B, D_HB))
        vs.append(vb.reshape(bsz, t_len, H_B, D_HB))
        ss.append(s_new)
    return xf.reshape(bsz, t_len, D_MODEL), jnp.stack(ks), jnp.stack(vs), jnp.stack(ss)


def kernel(x_prompt, x_sample, cache_sb_k, cache_sb_v, state_hgrn, ffn1_wg, ffn1_wu, ffn1_wd, ln1_g, ln1_b,
           w_in, lb_logits, hgrn_norm_g, w_branch_a, w_branch_b, w_out, ln2_g, ln2_b,
           ffn2_wg, ffn2_wu, ffn2_wd, ln3_g, ln3_b):
    depth = w_in.shape[0]

    def vec(p):
        return p[:, None, :]

    w = dict(
        ffn1_wg=ffn1_wg.astype(BF16), ffn1_wu=ffn1_wu.astype(BF16), ffn1_wd=ffn1_wd.astype(BF16),
        ln1_g=vec(ln1_g), ln1_b=vec(ln1_b), w_in=w_in.astype(BF16), lb_logits=lb_logits,
        hgrn_norm_g=vec(hgrn_norm_g), w_branch_a=w_branch_a.astype(BF16), w_branch_b=w_branch_b.astype(BF16),
        w_out=w_out.astype(BF16), ln2_g=vec(ln2_g), ln2_b=vec(ln2_b),
        ffn2_wg=ffn2_wg.astype(BF16), ffn2_wu=ffn2_wu.astype(BF16), ffn2_wd=ffn2_wd.astype(BF16),
        ln3_g=vec(ln3_g), ln3_b=vec(ln3_b))
    s0_prompt = jnp.zeros((depth, x_prompt.shape[0], H_A, K_A, V_A), F32)
    y_p, k_p, v_p, s_p = _trunk(x_prompt, s0_prompt, None, None, w)
    y_s, k_s, v_s, s_s = _trunk(x_sample, state_hgrn, cache_sb_k, cache_sb_v, w)
    return (y_p, y_s, k_p, v_p, s_p, k_s, v_s, s_s)
```
